```python
import math
import jax, jax.numpy as jnp
from jax import lax
import numpy as np

D_MODEL = 1024
BATCH = 16
SEQ = 4096
DEPTH = 1
DEC_BATCH = 8
DEC_SEQ = 4096
PAST_LEN = 128

GRID_W = 64
ROPE_THETA = 10000.0
Q_BLOCK = 128
RMS_EPS = 1e-6

GQA_HEADS = 8
GQA_KV_HEADS = 2
GQA_HEAD_DIM = 64
MLA_HEADS = 8
MLA_Q_LORA = 256
MLA_KV_LORA = 128
MLA_NOPE = 64
MLA_ROPE = 32
MLA_V = 64

D_GQA_OUT = GQA_HEADS * GQA_HEAD_DIM
D_MLA_OUT = MLA_HEADS * MLA_V
D_MIX = D_GQA_OUT + D_MLA_OUT

IN_SPLITS = (
    GQA_HEADS * GQA_HEAD_DIM,
    GQA_KV_HEADS * GQA_HEAD_DIM,
    GQA_KV_HEADS * GQA_HEAD_DIM,
    MLA_Q_LORA,
    MLA_KV_LORA,
    MLA_ROPE,
)
D_IN = sum(IN_SPLITS)

N_EXPERTS = 16
CAPACITY_FACTOR = 2
D_FF_EXPERT = 1024

kernel_name = "hybrid_gqa_mla_ec_moe_encoder"


def rms_norm(x, g):
    xf = x.astype(jnp.float32)
    y = xf * lax.rsqrt(jnp.mean(xf * xf, axis=-1, keepdims=True) + RMS_EPS)
    return (y * g.astype(jnp.float32)).astype(x.dtype)


def axial_cos_sin(seq_len, rot_dim):
    rows = seq_len // GRID_W
    half = rot_dim // 2
    inv_freq = ROPE_THETA ** (-jnp.arange(0, half, 2, dtype=jnp.float32) / half)
    row = jnp.repeat(jnp.arange(rows, dtype=jnp.float32), GRID_W)
    col = jnp.tile(jnp.arange(GRID_W, dtype=jnp.float32), rows)
    ang_r = row[:, None] * inv_freq[None, :]
    ang_c = col[:, None] * inv_freq[None, :]
    ang = jnp.concatenate([ang_r, ang_r, ang_c, ang_c], axis=-1)
    return jnp.cos(ang), jnp.sin(ang)


def apply_axial_rope(x, cos, sin):
    r = x.shape[-1]
    h = r // 2
    q = h // 2
    xf = x.astype(jnp.float32)
    rot = jnp.concatenate([-xf[..., q:h], xf[..., :q], -xf[..., h + q:], xf[..., h:h + q]], axis=-1)
    out = xf * cos[:, None, :] + rot * sin[:, None, :]
    return out.astype(x.dtype)


def block_attention(q, k, v, scale):
    b, s, h, dq = q.shape
    kv = k.shape[2]
    g = h // kv
    dv = v.shape[-1]
    nb = s // Q_BLOCK
    qb = q.reshape(b, nb, Q_BLOCK, kv, g, dq).transpose(1, 0, 2, 3, 4, 5)

    def one_block(qblk):
        sc = jnp.einsum('bqkgd,bskd->bkgqs', qblk, k).astype(jnp.float32) * scale
        p = jax.nn.softmax(sc, axis=-1).astype(v.dtype)
        return jnp.einsum('bkgqs,bskd->bqkgd', p, v)

    o = lax.map(one_block, qb)
    return o.transpose(1, 0, 2, 3, 4, 5).reshape(b, s, h * dv)


def expert_choice_moe(h, w_router, w_gate, w_up, w_down):
    b, s, d = h.shape
    n = b * s
    xf = h.reshape(n, d)
    logits = jnp.einsum('nd,de->ne', xf, w_router).astype(jnp.float32)
    aff = jax.nn.softmax(logits, axis=-1)
    cap = CAPACITY_FACTOR * n // N_EXPERTS
    gates, idx = lax.top_k(aff.T, cap)
    xe = xf[idx]
    hid = jax.nn.silu(jnp.einsum('ecd,edf->ecf', xe, w_gate)) * jnp.einsum('ecd,edf->ecf', xe, w_up)
    ye = jnp.einsum('ecf,efd->ecd', hid, w_down) * gates[..., None].astype(h.dtype)
    y = jnp.zeros_like(xf).at[idx.reshape(-1)].add(ye.reshape(-1, d))
    return y.reshape(b, s, d)


def encoder_layer(x, p):
    b, s, _ = x.shape
    cos_a, sin_a = axial_cos_sin(s, GQA_HEAD_DIM)
    cos_b, sin_b = axial_cos_sin(s, MLA_ROPE)

    hn = rms_norm(x, p['norm_attn'])
    proj = jnp.einsum('bsd,de->bse', hn, p['w_in'])
    offs = list(np.cumsum(IN_SPLITS)[:-1])
    gq, gk, gv, c_q, c_kv, k_pe = jnp.split(proj, offs, axis=-1)

    qa = rms_norm(gq.reshape(b, s, GQA_HEADS, GQA_HEAD_DIM), p['gqa_q_norm'])
    ka = rms_norm(gk.reshape(b, s, GQA_KV_HEADS, GQA_HEAD_DIM), p['gqa_k_norm'])
    va = gv.reshape(b, s, GQA_KV_HEADS, GQA_HEAD_DIM)
    qa = apply_axial_rope(qa, cos_a, sin_a)
    ka = apply_axial_rope(ka, cos_a, sin_a)
    oa = block_attention(qa, ka, va, 1.0 / math.sqrt(GQA_HEAD_DIM))

    cq = rms_norm(c_q, p['mla_q_a_norm'])
    qb = jnp.einsum('bsr,re->bse', cq, p['mla_w_uq']).reshape(b, s, MLA_HEADS, MLA_NOPE + MLA_ROPE)
    q_nope = rms_norm(qb[..., :MLA_NOPE], p['mla_q_nope_norm'])
    q_pe = rms_norm(qb[..., MLA_NOPE:], p['mla_q_pe_norm'])
    ckv = rms_norm(c_kv, p['mla_kv_a_norm'])
    kvb = jnp.einsum('bsr,re->bse', ckv, p['mla_w_ukv']).reshape(b, s, MLA_HEADS, MLA_NOPE + MLA_V)
    k_nope = rms_norm(kvb[..., :MLA_NOPE], p['mla_k_nope_norm'])
    vb = kvb[..., MLA_NOPE:]
    k_pe = rms_norm(k_pe.reshape(b, s, 1, MLA_ROPE), p['mla_k_pe_norm'])
    q_pe = apply_axial_rope(q_pe, cos_b, sin_b)
    k_pe = apply_axial_rope(k_pe, cos_b, sin_b)
    qm = jnp.concatenate([q_nope, q_pe], axis=-1)
    km = jnp.concatenate([k_nope, jnp.broadcast_to(k_pe, (b, s, MLA_HEADS, MLA_ROPE))], axis=-1)
    ob = block_attention(qm, km, vb, 1.0 / math.sqrt(MLA_NOPE + MLA_ROPE))

    o = jnp.concatenate([rms_norm(oa, p['out_norm_a']), rms_norm(ob, p['out_norm_b'])], axis=-1)
    x = x + jnp.einsum('bse,ed->bsd', o, p['w_o'])

    h2 = rms_norm(x, p['norm_ffn'])
    x = x + expert_choice_moe(h2, p['w_router'], p['w_gate'], p['w_up'], p['w_down'])
    return x


def setup_inputs(seed: int = 0) -> dict:
    key = jax.random.key(seed)
    ks = jax.random.split(key, 24)
    f32 = jnp.float32

    def nrm(k, shape, fan_in):
        return jax.random.normal(k, shape, f32) * (fan_in ** -0.5)

    def gain(k, n):
        return 1.0 + 0.02 * jax.random.normal(k, (n,), f32)

    return {
        'x_prompt': jax.random.normal(ks[0], (BATCH, SEQ, D_MODEL), f32),
        'x_sample': jax.random.normal(ks[1], (DEC_BATCH, DEC_SEQ, D_MODEL), f32),
        'norm_attn': gain(ks[2], D_MODEL),
        'w_in': nrm(ks[3], (D_MODEL, D_IN), D_MODEL),
        'gqa_q_norm': gain(ks[4], GQA_HEAD_DIM),
        'gqa_k_norm': gain(ks[5], GQA_HEAD_DIM),
        'mla_q_a_norm': gain(ks[6], MLA_Q_LORA),
        'mla_w_uq': nrm(ks[7], (MLA_Q_LORA, MLA_HEADS * (MLA_NOPE + MLA_ROPE)), MLA_Q_LORA),
        'mla_q_nope_norm': gain(ks[8], MLA_NOPE),
        'mla_q_pe_norm': gain(ks[9], MLA_ROPE),
        'mla_kv_a_norm': gain(ks[10], MLA_KV_LORA),
        'mla_w_ukv': nrm(ks[11], (MLA_KV_LORA, MLA_HEADS * (MLA_NOPE + MLA_V)), MLA_KV_LORA),
        'mla_k_nope_norm': gain(ks[12], MLA_NOPE),
        'mla_k_pe_norm': gain(ks[13], MLA_ROPE),
        'out_norm_a': gain(ks[14], D_GQA_OUT),
        'out_norm_b': gain(ks[15], D_MLA_OUT),
        'w_o': nrm(ks[16], (D_MIX, D_MODEL), D_MIX),
        'norm_ffn': gain(ks[17], D_MODEL),
        'w_router': nrm(ks[18], (D_MODEL, N_EXPERTS), D_MODEL),
        'w_gate': nrm(ks[19], (N_EXPERTS, D_MODEL, D_FF_EXPERT), D_MODEL),
        'w_up': nrm(ks[20], (N_EXPERTS, D_MODEL, D_FF_EXPERT), D_MODEL),
        'w_down': nrm(ks[21], (N_EXPERTS, D_FF_EXPERT, D_MODEL), D_FF_EXPERT),
    }


def reference(x_prompt, x_sample, norm_attn, w_in, gqa_q_norm, gqa_k_norm,
              mla_q_a_norm, mla_w_uq, mla_q_nope_norm, mla_q_pe_norm,
              mla_kv_a_norm, mla_w_ukv, mla_k_nope_norm, mla_k_pe_norm,
              out_norm_a, out_norm_b, w_o, norm_ffn, w_router, w_gate, w_up, w_down):
    p = {
        'norm_attn': norm_attn, 'w_in': w_in,
        'gqa_q_norm': gqa_q_norm, 'gqa_k_norm': gqa_k_norm,
        'mla_q_a_norm': mla_q_a_norm, 'mla_w_uq': mla_w_uq,
        'mla_q_nope_norm': mla_q_nope_norm, 'mla_q_pe_norm': mla_q_pe_norm,
        'mla_kv_a_norm': mla_kv_a_norm, 'mla_w_ukv': mla_w_ukv,
        'mla_k_nope_norm': mla_k_nope_norm, 'mla_k_pe_norm': mla_k_pe_norm,
        'out_norm_a': out_norm_a, 'out_norm_b': out_norm_b, 'w_o': w_o,
        'norm_ffn': norm_ffn, 'w_router': w_router,
        'w_gate': w_gate, 'w_up': w_up, 'w_down': w_down,
    }
    y_prompt = x_prompt
    y_sample = x_sample
    for _ in range(DEPTH):
        y_prompt = encoder_layer(y_prompt, p)
        y_sample = encoder_layer(y_sample, p)
    return (y_prompt, y_sample)
```

```python
import functools
import math

import jax
import jax.numpy as jnp
from jax import lax
from jax.experimental import pallas as pl
from jax.experimental.pallas import tpu as pltpu

D_MODEL = 1024
GRID_W = 64
ROPE_THETA = 10000.0
RMS_EPS = 1e-6
GQA_HEADS = 8
GQA_KV_HEADS = 2
GQA_HEAD_DIM = 64
MLA_HEADS = 8
MLA_Q_LORA = 256
MLA_KV_LORA = 128
MLA_NOPE = 64
MLA_ROPE = 32
MLA_V = 64
N_EXPERTS = 16
CAPACITY_FACTOR = 2
D_FF_EXPERT = 1024

LANES = 128
HALF = LANES // 2
D_IN_PAD = 1280
VMEM_LIMIT = 56 * 1024 * 1024

F32 = jnp.float32
BF16 = jnp.bfloat16


def _cparams(sem):
    return pltpu.CompilerParams(dimension_semantics=sem, vmem_limit_bytes=VMEM_LIMIT)


def _rope_tables(seq_len):
    rows = seq_len // GRID_W
    row = jnp.repeat(jnp.arange(rows, dtype=F32), GRID_W)
    col = jnp.tile(jnp.arange(GRID_W, dtype=F32), rows)

    def angles(rot_dim):
        half = rot_dim // 2
        inv = ROPE_THETA ** (-jnp.arange(0, half, 2, dtype=F32) / half)
        ar = row[:, None] * inv[None, :]
        ac = col[:, None] * inv[None, :]
        return jnp.concatenate([ar, ar, ac, ac], axis=-1)

    lane = jnp.arange(LANES)
    a64 = angles(GQA_HEAD_DIM)
    cos_a = jnp.tile(jnp.cos(a64), (1, 2))
    sin_a = jnp.tile(jnp.sin(a64), (1, 2))
    low_a = (lane % 32) < 16
    up_a = jnp.where(low_a, -sin_a, 0.0)
    dn_a = jnp.where(low_a, 0.0, sin_a)

    a32 = angles(MLA_ROPE)
    pad = ((0, 0), (MLA_NOPE, LANES - MLA_NOPE - MLA_ROPE))
    cos_b = jnp.pad(jnp.cos(a32) - 1.0, pad) + 1.0
    sin_b = jnp.pad(jnp.sin(a32), pad)
    low_b = ((lane - MLA_NOPE) % 16) < 8
    up_b = jnp.where(low_b, -sin_b, 0.0)
    dn_b = jnp.where(low_b, 0.0, sin_b)
    return jnp.stack([cos_a, up_a, dn_a, cos_b, up_b, dn_b])


def _segment_mean_matrix(groups):
    m = jnp.zeros((LANES, LANES), F32)
    for start, width in groups:
        m = m.at[start:start + width, start:start + width].set(1.0 / width)
    return m.astype(BF16)


def _prep_weights(p):
    w_in = p['w_in']
    kpe_cols = jnp.pad(w_in[:, 1152:1184], ((0, 0), (MLA_NOPE, LANES - MLA_NOPE - MLA_ROPE)))
    w_in_p = jnp.concatenate([w_in[:, :1152], kpe_cols], axis=1).astype(BF16)
    w_uq = p['mla_w_uq'].reshape(MLA_Q_LORA, MLA_HEADS, MLA_NOPE + MLA_ROPE)
    w_uq_p = jnp.pad(w_uq, ((0, 0), (0, 0), (0, LANES - MLA_NOPE - MLA_ROPE)))
    w_uq_p = w_uq_p.reshape(MLA_Q_LORA, MLA_HEADS * LANES).astype(BF16)
    w_ukv = p['mla_w_ukv'].reshape(MLA_KV_LORA, MLA_HEADS, MLA_NOPE + MLA_V)
    w_uk_p = jnp.pad(w_ukv[:, :, :MLA_NOPE], ((0, 0), (0, 0), (0, LANES - MLA_NOPE)))
    w_uk_p = w_uk_p.reshape(MLA_KV_LORA, MLA_HEADS * LANES).astype(BF16)
    w_uv = w_ukv[:, :, MLA_NOPE:].reshape(MLA_KV_LORA, MLA_HEADS * MLA_V).astype(BF16)

    def row(v):
        return v.reshape(1, -1).astype(F32)

    zeros32 = jnp.zeros((LANES - MLA_NOPE - MLA_ROPE,), F32)
    zeros64 = jnp.zeros((HALF,), F32)
    return dict(
        w_in=w_in_p, w_uq=w_uq_p, w_uk=w_uk_p, w_uv=w_uv,
        g_attn=row(p['norm_attn']),
        g_q=row(jnp.tile(p['gqa_q_norm'], 2)),
        g_k=row(jnp.tile(p['gqa_k_norm'], 2)),
        g_cq=row(p['mla_q_a_norm']),
        g_qm=row(jnp.concatenate([p['mla_q_nope_norm'], p['mla_q_pe_norm'], zeros32])),
        g_ckv=row(p['mla_kv_a_norm']),
        g_kn=row(jnp.concatenate([p['mla_k_nope_norm'], zeros64])),
        g_kpe=row(jnp.concatenate([zeros64, p['mla_k_pe_norm'], zeros32])),
        s_gqa=_segment_mean_matrix([(0, HALF), (HALF, HALF)]),
        s_mla=_segment_mean_matrix([(0, MLA_NOPE), (MLA_NOPE, MLA_ROPE)]),
        g_oa=row(p['out_norm_a']), g_ob=row(p['out_norm_b']),
        w_o=p['w_o'].astype(BF16),
        g_ffn=row(p['norm_ffn']),
        w_r=jnp.pad(p['w_router'], ((0, 0), (0, LANES - N_EXPERTS))).astype(BF16),
        w_gate=p['w_gate'].astype(BF16), w_up=p['w_up'].astype(BF16),
        w_down=p['w_down'].astype(BF16),
    )


def _rms(x, gain):
    ms = jnp.mean(x * x, axis=-1, keepdims=True)
    return x * lax.rsqrt(ms + RMS_EPS) * gain


def _segment_rms(x, seg_mat, gain):
    ms = jnp.dot((x * x).astype(BF16), seg_mat, preferred_element_type=F32)
    return x * lax.rsqrt(ms + RMS_EPS) * gain


def _rope(x, cos, up, dn, quarter):
    return (x * cos + pltpu.roll(x, LANES - quarter, 1) * up + pltpu.roll(x, quarter, 1) * dn)


def _pre_attn_kernel(x_ref, tab_ref, g_attn, w_in, g_q, g_k, g_cq, w_uq, g_qm, g_ckv, w_uk, w_uv,
                     g_kn, g_kpe, s_gqa, s_mla,
                     q_ref, k_ref, v_ref, qm_ref, km_ref, vm_ref):
    cos_a, up_a, dn_a = tab_ref[0], tab_ref[1], tab_ref[2]
    cos_b, up_b, dn_b = tab_ref[3], tab_ref[4], tab_ref[5]
    hn = _rms(x_ref[...], g_attn[...]).astype(BF16)
    proj = jnp.dot(hn, w_in[...], preferred_element_type=F32)

    sg = s_gqa[...]
    sm = s_mla[...]
    q_scale = 1.0 / math.sqrt(GQA_HEAD_DIM)
    for c in range(GQA_HEADS // 2):
        xc = _segment_rms(proj[:, c * LANES:(c + 1) * LANES], sg, g_q[...])
        q_ref[c] = (_rope(xc, cos_a, up_a, dn_a, 16) * q_scale).astype(BF16)

    low = lax.broadcasted_iota(jnp.int32, (1, LANES), 1) < HALF
    k01 = _rope(_segment_rms(proj[:, 512:640], sg, g_k[...]), cos_a, up_a, dn_a, 16)
    k10 = pltpu.roll(k01, HALF, 1)
    k_ref[0] = jnp.where(low, k01, 0.0).astype(BF16)
    k_ref[1] = jnp.where(low, 0.0, k10).astype(BF16)
    k_ref[2] = jnp.where(low, k10, 0.0).astype(BF16)
    k_ref[3] = jnp.where(low, 0.0, k01).astype(BF16)
    v01 = proj[:, 640:768]
    v_ref[0] = v01.astype(BF16)
    v_ref[1] = pltpu.roll(v01, HALF, 1).astype(BF16)

    cq = _rms(proj[:, 768:1024], g_cq[...]).astype(BF16)
    qb = jnp.dot(cq, w_uq[...], preferred_element_type=F32)
    m_scale = 1.0 / math.sqrt(MLA_NOPE + MLA_ROPE)
    ckv = _rms(proj[:, 1024:1152], g_ckv[...]).astype(BF16)
    kn = jnp.dot(ckv, w_uk[...], preferred_element_type=F32)
    vv = jnp.dot(ckv, w_uv[...], preferred_element_type=F32)
    kpe = _rope(_segment_rms(proj[:, 1152:1280], sm, g_kpe[...]), cos_b, up_b, dn_b, 8)
    for h in range(MLA_HEADS):
        qh = _segment_rms(qb[:, h * LANES:(h + 1) * LANES], sm, g_qm[...])
        qm_ref[h] = (_rope(qh, cos_b, up_b, dn_b, 8) * m_scale).astype(BF16)
        kh = _segment_rms(kn[:, h * LANES:(h + 1) * LANES], sm, g_kn[...])
        km_ref[h] = (kh + kpe).astype(BF16)
    for c in range(MLA_HEADS // 2):
        vm_ref[c] = vv[:, c * LANES:(c + 1) * LANES].astype(BF16)


def _full(shape):
    return pl.BlockSpec(shape, lambda *_: (0,) * len(shape))


def _pre_attention(x, tables, w, tm):
    b, s, d = x.shape
    nt = s // tm
    grid = (b, nt)

    def head_major(n):
        return pl.BlockSpec((None, n, tm, LANES), lambda bi, i: (bi, 0, i, 0))

    weights = [w['g_attn'], w['w_in'], w['g_q'], w['g_k'], w['g_cq'], w['w_uq'], w['g_qm'],
               w['g_ckv'], w['w_uk'], w['w_uv'], w['g_kn'], w['g_kpe'], w['s_gqa'], w['s_mla']]
    in_specs = [pl.BlockSpec((None, tm, d), lambda bi, i: (bi, i, 0)),
                pl.BlockSpec((6, tm, LANES), lambda bi, i: (0, i, 0))]
    in_specs += [_full(a.shape) for a in weights]
    counts = (4, 4, 2, 8, 8, 4)
    out_shape = [jax.ShapeDtypeStruct((b, n, s, LANES), BF16) for n in counts]
    out_specs = [head_major(n) for n in counts]
    return pl.pallas_call(
        _pre_attn_kernel, grid=grid, in_specs=in_specs, out_specs=out_specs, out_shape=out_shape,
        compiler_params=_cparams(("parallel", "parallel")),
    )(x, tables, *weights)


def _attn_kernel(qe_ref, qo_ref, ke_ref, ko_ref, ve_ref, vo_ref, o_ref, m_sc, l_sc, acc_sc, *, tk):
    seq = ke_ref.shape[0]
    halves = []
    for q_ref, k_ref, v_ref in ((qe_ref, ke_ref, ve_ref), (qo_ref, ko_ref, vo_ref)):
        q = q_ref[...]
        m_sc[...] = jnp.full(m_sc.shape, -jnp.inf, F32)
        l_sc[...] = jnp.zeros(l_sc.shape, F32)
        acc_sc[...] = jnp.zeros(acc_sc.shape, F32)

        def body(j, carry, q=q, k_ref=k_ref, v_ref=v_ref):
            start = pl.multiple_of(j * tk, tk)
            kc = k_ref[pl.ds(start, tk), :]
            vc = v_ref[pl.ds(start, tk), :]
            s = lax.dot_general(q, kc, (((1,), (1,)), ((), ())), preferred_element_type=F32)
            m_prev = m_sc[...]
            m_new = jnp.maximum(m_prev, jnp.max(s, axis=-1, keepdims=True))
            alpha = jnp.exp(m_prev - m_new)
            p = jnp.exp(s - m_new)
            l_sc[...] = alpha * l_sc[...] + jnp.sum(p, axis=-1, keepdims=True)
            acc_sc[...] = alpha * acc_sc[...] + jnp.dot(p.astype(BF16), vc,
                                                        preferred_element_type=F32)
            m_sc[...] = m_new
            return carry

        lax.fori_loop(0, seq // tk, body, 0)
        halves.append(acc_sc[...] / l_sc[...])
    low = lax.broadcasted_iota(jnp.int32, (1, LANES), 1) < HALF
    o_ref[...] = jnp.where(low, halves[0], halves[1])


def _attention(q_arr, k_arr, v_arr, maps, tq, tk):
    b, _, s, _ = q_arr.shape
    n_pairs = 4
    grid = (b, n_pairs, s // tq)
    qe, qo, ke, ko, ve, vo = maps

    def qspec(f):
        return pl.BlockSpec((None, None, tq, LANES), lambda bi, p, i: (bi, f(p), i, 0))

    def kvspec(f):
        return pl.BlockSpec((None, None, s, LANES), lambda bi, p, i: (bi, f(p), 0, 0))

    return pl.pallas_call(
        functools.partial(_attn_kernel, tk=tk), grid=grid,
        in_specs=[qspec(qe), qspec(qo), kvspec(ke), kvspec(ko), kvspec(ve), kvspec(vo)],
        out_specs=pl.BlockSpec((None, tq, LANES), lambda bi, p, i: (bi, i, p)),
        out_shape=jax.ShapeDtypeStruct((b, s, n_pairs * LANES), F32),
        scratch_shapes=[pltpu.VMEM((tq, 1), F32), pltpu.VMEM((tq, 1), F32),
                        pltpu.VMEM((tq, LANES), F32)],
        compiler_params=_cparams(("parallel", "parallel", "parallel")),
    )(q_arr, q_arr, k_arr, k_arr, v_arr, v_arr)


_GQA_MAPS = (lambda p: p, lambda p: p,
             lambda p: 2 * (p // 2), lambda p: 2 * (p // 2) + 1,
             lambda p: p // 2, lambda p: 1 - p // 2)
_MLA_MAPS = (lambda p: 2 * p, lambda p: 2 * p + 1,
             lambda p: 2 * p, lambda p: 2 * p + 1,
             lambda p: p, lambda p: p)


def _post_attn_kernel(x_ref, oa_ref, ob_ref, g_oa, g_ob, w_o, g_ffn, w_r,
                      x1_ref, h2_ref, afft_ref):
    na = _rms(oa_ref[...], g_oa[...]).astype(BF16)
    nb = _rms(ob_ref[...], g_ob[...]).astype(BF16)
    half = na.shape[1]
    mix = (jnp.dot(na, w_o[:half, :], preferred_element_type=F32)
           + jnp.dot(nb, w_o[half:, :], preferred_element_type=F32))
    x1 = x_ref[...] + mix
    x1_ref[...] = x1
    h2 = _rms(x1, g_ffn[...]).astype(BF16)
    h2_ref[...] = h2
    logits = jnp.dot(h2, w_r[...], preferred_element_type=F32)
    valid = lax.broadcasted_iota(jnp.int32, (1, LANES), 1) < N_EXPERTS
    logits = jnp.where(valid, logits, -jnp.inf)
    e = jnp.exp(logits - jnp.max(logits, axis=-1, keepdims=True))
    aff = e / jnp.sum(e, axis=-1, keepdims=True)
    afft_ref[...] = jnp.transpose(aff)[:N_EXPERTS, :]


def _post_attention(x, oa, ob, w, tm):
    b, s, d = x.shape
    nt = s // tm
    tok = lambda width: pl.BlockSpec((None, tm, width), lambda bi, i: (bi, i, 0))
    weights = [w['g_oa'], w['g_ob'], w['w_o'], w['g_ffn'], w['w_r']]
    return pl.pallas_call(
        _post_attn_kernel, grid=(b, nt),
        in_specs=[tok(d), tok(oa.shape[2]), tok(ob.shape[2])] + [_full(a.shape) for a in weights],
        out_specs=[tok(d), tok(d),
                   pl.BlockSpec((N_EXPERTS, tm), lambda bi, i: (0, bi * nt + i))],
        out_shape=[jax.ShapeDtypeStruct((b, s, d), F32), jax.ShapeDtypeStruct((b, s, d), BF16),
                   jax.ShapeDtypeStruct((N_EXPERTS, b * s), F32)],
        compiler_params=_cparams(("parallel", "parallel")),
    )(x, oa, ob, *weights)


def _expert_kernel(xe_ref, gate_ref, wg_ref, wu_ref, wd_ref, ye_ref):
    xe = xe_ref[...]
    a = jnp.dot(xe, wg_ref[...], preferred_element_type=F32)
    u = jnp.dot(xe, wu_ref[...], preferred_element_type=F32)
    hid = (a * jax.nn.sigmoid(a) * u).astype(BF16)
    ye = jnp.dot(hid, wd_ref[...], preferred_element_type=F32)
    ye_ref[...] = (ye * gate_ref[...]).astype(ye_ref.dtype)


def _expert_ffn(xe, gates, w, tc, out_dtype):
    e, c, d = xe.shape
    f = w['w_gate'].shape[2]
    slot = lambda width: pl.BlockSpec((None, tc, width), lambda ei, ci: (ei, ci, 0))
    wspec = lambda r, cc: pl.BlockSpec((None, r, cc), lambda ei, ci: (ei, 0, 0))
    return pl.pallas_call(
        _expert_kernel, grid=(e, c // tc),
        in_specs=[slot(d), slot(1), wspec(d, f), wspec(d, f), wspec(f, d)],
        out_specs=slot(d),
        out_shape=jax.ShapeDtypeStruct((e, c, d), out_dtype),
        compiler_params=_cparams(("parallel", "parallel")),
    )(xe, gates, w['w_gate'], w['w_up'], w['w_down'])


def _tile(n, pref):
    t = min(n, pref)
    assert n % t == 0
    return t


def _encoder_layer(x, w):
    b, s, d = x.shape
    n = b * s
    tm = _tile(s, 256)
    tables = _rope_tables(s)
    q, k, v, qm, km, vm = _pre_attention(x, tables, w, tm)
    tq = _tile(s, 512)
    tk = _tile(s, 512)
    oa = _attention(q, k, v, _GQA_MAPS, tq, tk)
    ob = _attention(qm, km, vm, _MLA_MAPS, tq, tk)
    x1, h2, afft = _post_attention(x, oa, ob, w, _tile(s, 512))

    cap = CAPACITY_FACTOR * n // N_EXPERTS
    gates, idx = lax.top_k(afft, cap)
    xe = h2.reshape(n, d)[idx]
    ye = _expert_ffn(xe, gates[..., None], w, _tile(cap, 512), F32)
    y = x1.reshape(n, d).at[idx.reshape(-1)].add(ye.reshape(-1, d))
    return y.reshape(b, s, d)


def kernel(x_prompt, x_sample, norm_attn, w_in, gqa_q_norm, gqa_k_norm, mla_q_a_norm, mla_w_uq,
           mla_q_nope_norm, mla_q_pe_norm, mla_kv_a_norm, mla_w_ukv, mla_k_nope_norm,
           mla_k_pe_norm, out_norm_a, out_norm_b, w_o, norm_ffn, w_router, w_gate, w_up, w_down):
    p = dict(norm_attn=norm_attn, w_in=w_in, gqa_q_norm=gqa_q_norm, gqa_k_norm=gqa_k_norm,
             mla_q_a_norm=mla_q_a_norm, mla_w_uq=mla_w_uq, mla_q_nope_norm=mla_q_nope_norm,
             mla_q_pe_norm=mla_q_pe_norm, mla_kv_a_norm=mla_kv_a_norm, mla_w_ukv=mla_w_ukv,
             mla_k_nope_norm=mla_k_nope_norm, mla_k_pe_norm=mla_k_pe_norm,
             out_norm_a=out_norm_a, out_norm_b=out_norm_b, w_o=w_o, norm_ffn=norm_ffn,
             w_router=w_router, w_gate=w_gate, w_up=w_up, w_down=w_down)
    w = _prep_weights(p)
    return (_encoder_layer(x_prompt, w), _encoder_layer(x_sample, w))
```

```python
import functools
import math

import jax
import jax.numpy as jnp
from jax import lax
from jax.experimental import pallas as pl
from jax.experimental.pallas import tpu as pltpu

D_MODEL = 1024
GRID_W = 64
ROPE_THETA = 10000.0
RMS_EPS = 1e-6
GQA_HEADS = 8
GQA_KV_HEADS = 2
GQA_HEAD_DIM = 64
MLA_HEADS = 8
MLA_Q_LORA = 256
MLA_KV_LORA = 128
MLA_NOPE = 64
MLA_ROPE = 32
MLA_V = 64
N_EXPERTS = 16
CAPACITY_FACTOR = 2
D_FF_EXPERT = 1024

LANES = 128
HALF = LANES // 2
D_IN_PAD = 1280
VMEM_LIMIT = 56 * 1024 * 1024
LOG2E = math.log2(math.e)

F32 = jnp.float32
BF16 = jnp.bfloat16


def _cparams(sem):
    return pltpu.CompilerParams(dimension_semantics=sem, vmem_limit_bytes=VMEM_LIMIT)


def _rope_tables(seq_len):
    rows = seq_len // GRID_W
    row = jnp.repeat(jnp.arange(rows, dtype=F32), GRID_W)
    col = jnp.tile(jnp.arange(GRID_W, dtype=F32), rows)

    def angles(rot_dim):
        half = rot_dim // 2
        inv = ROPE_THETA ** (-jnp.arange(0, half, 2, dtype=F32) / half)
        ar = row[:, None] * inv[None, :]
        ac = col[:, None] * inv[None, :]
        return jnp.concatenate([ar, ar, ac, ac], axis=-1)

    lane = jnp.arange(LANES)
    a64 = angles(GQA_HEAD_DIM)
    cos_a = jnp.tile(jnp.cos(a64), (1, 2))
    sin_a = jnp.tile(jnp.sin(a64), (1, 2))
    low_a = (lane % 32) < 16
    up_a = jnp.where(low_a, -sin_a, 0.0)
    dn_a = jnp.where(low_a, 0.0, sin_a)

    a32 = angles(MLA_ROPE)
    pad = ((0, 0), (MLA_NOPE, LANES - MLA_NOPE - MLA_ROPE))
    cos_b = jnp.pad(jnp.cos(a32) - 1.0, pad) + 1.0
    sin_b = jnp.pad(jnp.sin(a32), pad)
    low_b = ((lane - MLA_NOPE) % 16) < 8
    up_b = jnp.where(low_b, -sin_b, 0.0)
    dn_b = jnp.where(low_b, 0.0, sin_b)
    return jnp.stack([cos_a, up_a, dn_a, cos_b, up_b, dn_b])


def _segment_mean_matrix(groups):
    m = jnp.zeros((LANES, LANES), F32)
    for start, width in groups:
        m = m.at[start:start + width, start:start + width].set(1.0 / width)
    return m.astype(BF16)


def _prep_weights(p):
    w_in = p['w_in']
    kpe_cols = jnp.pad(w_in[:, 1152:1184], ((0, 0), (MLA_NOPE, LANES - MLA_NOPE - MLA_ROPE)))
    w_in_p = jnp.concatenate([w_in[:, :1152], kpe_cols], axis=1).astype(BF16)
    w_uq = p['mla_w_uq'].reshape(MLA_Q_LORA, MLA_HEADS, MLA_NOPE + MLA_ROPE)
    w_uq_p = jnp.pad(w_uq, ((0, 0), (0, 0), (0, LANES - MLA_NOPE - MLA_ROPE)))
    w_uq_p = w_uq_p.reshape(MLA_Q_LORA, MLA_HEADS * LANES).astype(BF16)
    w_ukv = p['mla_w_ukv'].reshape(MLA_KV_LORA, MLA_HEADS, MLA_NOPE + MLA_V)
    w_uk_p = jnp.pad(w_ukv[:, :, :MLA_NOPE], ((0, 0), (0, 0), (0, LANES - MLA_NOPE)))
    w_uk_p = w_uk_p.reshape(MLA_KV_LORA, MLA_HEADS * LANES).astype(BF16)
    w_uv = w_ukv[:, :, MLA_NOPE:].reshape(MLA_KV_LORA, MLA_HEADS * MLA_V).astype(BF16)

    def row(v):
        return v.reshape(1, -1).astype(F32)

    zeros32 = jnp.zeros((LANES - MLA_NOPE - MLA_ROPE,), F32)
    zeros64 = jnp.zeros((HALF,), F32)
    return dict(
        w_in=w_in_p, w_uq=w_uq_p, w_uk=w_uk_p, w_uv=w_uv,
        g_attn=row(p['norm_attn']),
        g_q=row(jnp.tile(p['gqa_q_norm'], 2)),
        g_k=row(jnp.tile(p['gqa_k_norm'], 2)),
        g_cq=row(p['mla_q_a_norm']),
        g_qm=row(jnp.concatenate([p['mla_q_nope_norm'], p['mla_q_pe_norm'], zeros32])),
        g_ckv=row(p['mla_kv_a_norm']),
        g_kn=row(jnp.concatenate([p['mla_k_nope_norm'], zeros64])),
        g_kpe=row(jnp.concatenate([zeros64, p['mla_k_pe_norm'], zeros32])),
        s_gqa=_segment_mean_matrix([(0, HALF), (HALF, HALF)]),
        s_mla=_segment_mean_matrix([(0, MLA_NOPE), (MLA_NOPE, MLA_ROPE)]),
        g_oa=row(p['out_norm_a']), g_ob=row(p['out_norm_b']),
        w_o=p['w_o'].astype(BF16),
        g_ffn=row(p['norm_ffn']),
        w_r=jnp.pad(p['w_router'], ((0, 0), (0, LANES - N_EXPERTS))).astype(BF16),
        w_gate=p['w_gate'].astype(BF16), w_up=p['w_up'].astype(BF16),
        w_down=p['w_down'].astype(BF16),
    )


def _rms(x, gain):
    ms = jnp.mean(x * x, axis=-1, keepdims=True)
    return x * lax.rsqrt(ms + RMS_EPS) * gain


def _segment_rms(x, seg_mat, gain):
    ms = jnp.dot((x * x).astype(BF16), seg_mat, preferred_element_type=F32)
    return x * lax.rsqrt(ms + RMS_EPS) * gain


def _rope(x, cos, up, dn, quarter):
    return (x * cos + pltpu.roll(x, LANES - quarter, 1) * up + pltpu.roll(x, quarter, 1) * dn)


def _pre_attn_kernel(x_ref, tab_ref, g_attn, w_in, g_q, g_k, g_cq, w_uq, g_qm, g_ckv, w_uk, w_uv,
                     g_kn, g_kpe, s_gqa, s_mla,
                     q_ref, k_ref, v_ref, qm_ref, km_ref, vm_ref):
    cos_a, up_a, dn_a = tab_ref[0], tab_ref[1], tab_ref[2]
    cos_b, up_b, dn_b = tab_ref[3], tab_ref[4], tab_ref[5]
    hn = _rms(x_ref[...], g_attn[...]).astype(BF16)
    proj = jnp.dot(hn, w_in[...], preferred_element_type=F32)

    sg = s_gqa[...]
    sm = s_mla[...]
    q_scale = LOG2E / math.sqrt(GQA_HEAD_DIM)
    for c in range(GQA_HEADS // 2):
        xc = _segment_rms(proj[:, c * LANES:(c + 1) * LANES], sg, g_q[...])
        q_ref[c] = (_rope(xc, cos_a, up_a, dn_a, 16) * q_scale).astype(BF16)

    low = lax.broadcasted_iota(jnp.int32, (1, LANES), 1) < HALF
    k01 = _rope(_segment_rms(proj[:, 512:640], sg, g_k[...]), cos_a, up_a, dn_a, 16)
    k10 = pltpu.roll(k01, HALF, 1)
    k_ref[0] = jnp.where(low, k01, 0.0).astype(BF16)
    k_ref[1] = jnp.where(low, 0.0, k10).astype(BF16)
    k_ref[2] = jnp.where(low, k10, 0.0).astype(BF16)
    k_ref[3] = jnp.where(low, 0.0, k01).astype(BF16)
    v01 = proj[:, 640:768]
    v_ref[0] = jnp.transpose(v01).astype(BF16)
    v_ref[1] = jnp.transpose(pltpu.roll(v01, HALF, 1)).astype(BF16)

    cq = _rms(proj[:, 768:1024], g_cq[...]).astype(BF16)
    qb = jnp.dot(cq, w_uq[...], preferred_element_type=F32)
    m_scale = LOG2E / math.sqrt(MLA_NOPE + MLA_ROPE)
    ckv = _rms(proj[:, 1024:1152], g_ckv[...]).astype(BF16)
    kn = jnp.dot(ckv, w_uk[...], preferred_element_type=F32)
    vv = jnp.dot(ckv, w_uv[...], preferred_element_type=F32)
    kpe = _rope(_segment_rms(proj[:, 1152:1280], sm, g_kpe[...]), cos_b, up_b, dn_b, 8)
    for h in range(MLA_HEADS):
        qh = _segment_rms(qb[:, h * LANES:(h + 1) * LANES], sm, g_qm[...])
        qm_ref[h] = (_rope(qh, cos_b, up_b, dn_b, 8) * m_scale).astype(BF16)
        kh = _segment_rms(kn[:, h * LANES:(h + 1) * LANES], sm, g_kn[...])
        km_ref[h] = (kh + kpe).astype(BF16)
    for c in range(MLA_HEADS // 2):
        vm_ref[c] = jnp.transpose(vv[:, c * LANES:(c + 1) * LANES]).astype(BF16)


def _full(shape):
    return pl.BlockSpec(shape, lambda *_: (0,) * len(shape))


def _pre_attention(x, tables, w, tm):
    b, s, d = x.shape
    nt = s // tm
    grid = (b, nt)

    def head_major(n):
        return pl.BlockSpec((None, n, tm, LANES), lambda bi, i: (bi, 0, i, 0))

    weights = [w['g_attn'], w['w_in'], w['g_q'], w['g_k'], w['g_cq'], w['w_uq'], w['g_qm'],
               w['g_ckv'], w['w_uk'], w['w_uv'], w['g_kn'], w['g_kpe'], w['s_gqa'], w['s_mla']]
    in_specs = [pl.BlockSpec((None, tm, d), lambda bi, i: (bi, i, 0)),
                pl.BlockSpec((6, tm, LANES), lambda bi, i: (0, i, 0))]
    in_specs += [_full(a.shape) for a in weights]
    def head_major_t(n):
        return pl.BlockSpec((None, n, None, LANES, tm), lambda bi, i: (bi, 0, i, 0, 0))

    counts = (4, 4, 2, 8, 8, 4)
    transposed = (False, False, True, False, False, True)
    out_shape = [jax.ShapeDtypeStruct((b, n, nt, LANES, tm) if t else (b, n, s, LANES), BF16)
                 for n, t in zip(counts, transposed)]
    out_specs = [head_major_t(n) if t else head_major(n) for n, t in zip(counts, transposed)]
    return pl.pallas_call(
        _pre_attn_kernel, grid=grid, in_specs=in_specs, out_specs=out_specs, out_shape=out_shape,
        compiler_params=_cparams(("parallel", "parallel")),
    )(x, tables, *weights)


def _attn_kernel(qe_ref, qo_ref, ke_ref, ko_ref, ve_ref, vo_ref, o_ref, m_sc, l_sc, acc_sc, *, tk):
    n_chunks = ve_ref.shape[0]
    refs = ((qe_ref, ke_ref, ve_ref), (qo_ref, ko_ref, vo_ref))
    qts = [jnp.transpose(q_ref[...].astype(F32)).astype(BF16) for q_ref, _, _ in refs]
    m_sc[...] = jnp.full(m_sc.shape, -jnp.inf, F32)
    l_sc[...] = jnp.zeros(l_sc.shape, F32)
    acc_sc[...] = jnp.zeros(acc_sc.shape, F32)

    def body(j, carry):
        start = pl.multiple_of(j * tk, tk)
        for h, (_, k_ref, v_ref) in enumerate(refs):
            st = jnp.dot(k_ref[pl.ds(start, tk), :], qts[h], preferred_element_type=F32)
            m_prev = m_sc[h]
            m_new = jnp.maximum(m_prev, jnp.max(st, axis=0, keepdims=True))
            alpha = jnp.exp2(m_prev - m_new)
            p = jnp.exp2(st - m_new)
            l_sc[h] = alpha * l_sc[h] + jnp.sum(p, axis=0, keepdims=True)
            acc_sc[h] = alpha * acc_sc[h] + jnp.dot(v_ref[j], p.astype(BF16),
                                                    preferred_element_type=F32)
            m_sc[h] = m_new
        return carry

    lax.fori_loop(0, n_chunks, body, 0)
    ot = jnp.concatenate([(acc_sc[0] / l_sc[0])[:HALF], (acc_sc[1] / l_sc[1])[HALF:]], axis=0)
    o_ref[...] = jnp.transpose(ot)


def _attention(q_arr, k_arr, vt_arr, maps, tq):
    b, _, s, _ = q_arr.shape
    n_pairs = 4
    n_chunks, _, tk = vt_arr.shape[2:]
    grid = (b, n_pairs, s // tq)
    qe, qo, ke, ko, ve, vo = maps

    def qspec(f):
        return pl.BlockSpec((None, None, tq, LANES), lambda bi, p, i: (bi, f(p), i, 0))

    def kspec(f):
        return pl.BlockSpec((None, None, s, LANES), lambda bi, p, i: (bi, f(p), 0, 0))

    def vspec(f):
        return pl.BlockSpec((None, None, n_chunks, LANES, tk),
                            lambda bi, p, i: (bi, f(p), 0, 0, 0))

    return pl.pallas_call(
        functools.partial(_attn_kernel, tk=tk), grid=grid,
        in_specs=[qspec(qe), qspec(qo), kspec(ke), kspec(ko), vspec(ve), vspec(vo)],
        out_specs=pl.BlockSpec((None, tq, LANES), lambda bi, p, i: (bi, i, p)),
        out_shape=jax.ShapeDtypeStruct((b, s, n_pairs * LANES), F32),
        scratch_shapes=[pltpu.VMEM((2, 1, tq), F32), pltpu.VMEM((2, 1, tq), F32),
                        pltpu.VMEM((2, LANES, tq), F32)],
        compiler_params=_cparams(("parallel", "parallel", "parallel")),
    )(q_arr, q_arr, k_arr, k_arr, vt_arr, vt_arr)


_GQA_MAPS = (lambda p: p, lambda p: p,
             lambda p: 2 * (p // 2), lambda p: 2 * (p // 2) + 1,
             lambda p: p // 2, lambda p: 1 - p // 2)
_MLA_MAPS = (lambda p: 2 * p, lambda p: 2 * p + 1,
             lambda p: 2 * p, lambda p: 2 * p + 1,
             lambda p: p, lambda p: p)


def _post_attn_kernel(x_ref, oa_ref, ob_ref, g_oa, g_ob, w_o, g_ffn, w_r,
                      x1_ref, h2_ref, afft_ref):
    na = _rms(oa_ref[...], g_oa[...]).astype(BF16)
    nb = _rms(ob_ref[...], g_ob[...]).astype(BF16)
    half = na.shape[1]
    mix = (jnp.dot(na, w_o[:half, :], preferred_element_type=F32)
           + jnp.dot(nb, w_o[half:, :], preferred_element_type=F32))
    x1 = x_ref[...] + mix
    x1_ref[...] = x1
    h2 = _rms(x1, g_ffn[...]).astype(BF16)
    h2_ref[...] = h2
    logits = jnp.dot(h2, w_r[...], preferred_element_type=F32)
    valid = lax.broadcasted_iota(jnp.int32, (1, LANES), 1) < N_EXPERTS
    logits = jnp.where(valid, logits, -jnp.inf)
    e = jnp.exp(logits - jnp.max(logits, axis=-1, keepdims=True))
    aff = e / jnp.sum(e, axis=-1, keepdims=True)
    afft_ref[...] = jnp.transpose(aff)[:N_EXPERTS, :]


def _post_attention(x, oa, ob, w, tm):
    b, s, d = x.shape
    nt = s // tm
    tok = lambda width: pl.BlockSpec((None, tm, width), lambda bi, i: (bi, i, 0))
    weights = [w['g_oa'], w['g_ob'], w['w_o'], w['g_ffn'], w['w_r']]
    return pl.pallas_call(
        _post_attn_kernel, grid=(b, nt),
        in_specs=[tok(d), tok(oa.shape[2]), tok(ob.shape[2])] + [_full(a.shape) for a in weights],
        out_specs=[tok(d), tok(d),
                   pl.BlockSpec((N_EXPERTS, tm), lambda bi, i: (0, bi * nt + i))],
        out_shape=[jax.ShapeDtypeStruct((b, s, d), F32), jax.ShapeDtypeStruct((b, s, d), BF16),
                   jax.ShapeDtypeStruct((N_EXPERTS, b * s), F32)],
        compiler_params=_cparams(("parallel", "parallel")),
    )(x, oa, ob, *weights)


def _expert_kernel(xe_ref, gate_ref, wg_ref, wu_ref, wd_ref, ye_ref):
    xe = xe_ref[...]
    a = jnp.dot(xe, wg_ref[...], preferred_element_type=F32)
    u = jnp.dot(xe, wu_ref[...], preferred_element_type=F32)
    hid = (a * jax.nn.sigmoid(a) * u).astype(BF16)
    ye = jnp.dot(hid, wd_ref[...], preferred_element_type=F32)
    ye_ref[...] = (ye * gate_ref[...]).astype(ye_ref.dtype)


def _expert_ffn(xe, gates, w, tc, out_dtype):
    e, c, d = xe.shape
    f = w['w_gate'].shape[2]
    slot = lambda width: pl.BlockSpec((None, tc, width), lambda ei, ci: (ei, ci, 0))
    wspec = lambda r, cc: pl.BlockSpec((None, r, cc), lambda ei, ci: (ei, 0, 0))
    return pl.pallas_call(
        _expert_kernel, grid=(e, c // tc),
        in_specs=[slot(d), slot(1), wspec(d, f), wspec(d, f), wspec(f, d)],
        out_specs=slot(d),
        out_shape=jax.ShapeDtypeStruct((e, c, d), out_dtype),
        compiler_params=_cparams(("parallel", "parallel")),
    )(xe, gates, w['w_gate'], w['w_up'], w['w_down'])


def _tile(n, pref):
    t = min(n, pref)
    assert n % t == 0
    return t


def _encoder_layer(x, w):
    b, s, d = x.shape
    n = b * s
    tk = _tile(s, 1024)
    tables = _rope_tables(s)
    q, k, vt, qm, km, vmt = _pre_attention(x, tables, w, tk)
    tq = _tile(s, 1024)
    oa = _attention(q, k, vt, _GQA_MAPS, tq)
    ob = _attention(qm, km, vmt, _MLA_MAPS, tq)
    x1, h2, afft = _post_attention(x, oa, ob, w, _tile(s, 512))

    cap = CAPACITY_FACTOR * n // N_EXPERTS
    gates, idx = lax.top_k(afft, cap)
    xe = h2.reshape(n, d)[idx]
    ye = _expert_ffn(xe, gates[..., None], w, _tile(cap, 512), F32)
    y = x1.reshape(n, d).at[idx.reshape(-1)].add(ye.reshape(-1, d))
    return y.reshape(b, s, d)


def kernel(x_prompt, x_sample, norm_attn, w_in, gqa_q_norm, gqa_k_norm, mla_q_a_norm, mla_w_uq,
           mla_q_nope_norm, mla_q_pe_norm, mla_kv_a_norm, mla_w_ukv, mla_k_nope_norm,
           mla_k_pe_norm, out_norm_a, out_norm_b, w_o, norm_ffn, w_router, w_gate, w_up, w_down):
    p = dict(norm_attn=norm_attn, w_in=w_in, gqa_q_norm=gqa_q_norm, gqa_k_norm=gqa_k_norm,
             mla_q_a_norm=mla_q_a_norm, mla_w_uq=mla_w_uq, mla_q_nope_norm=mla_q_nope_norm,
             mla_q_pe_norm=mla_q_pe_norm, mla_kv_a_norm=mla_kv_a_norm, mla_w_ukv=mla_w_ukv,
             mla_k_nope_norm=mla_k_nope_norm, mla_k_pe_norm=mla_k_pe_norm,
             out_norm_a=out_norm_a, out_norm_b=out_norm_b, w_o=w_o, norm_ffn=norm_ffn,
             w_router=w_router, w_gate=w_gate, w_up=w_up, w_down=w_down)
    w = _prep_weights(p)
    return (_encoder_layer(x_prompt, w), _encoder_layer(x_sample, w))
```

```python
import functools
import math

import jax
import jax.numpy as jnp
from jax import lax
from jax.experimental import pallas as pl
from jax.experimental.pallas import tpu as pltpu

D_MODEL = 1024
GRID_W = 64
ROPE_THETA = 10000.0
RMS_EPS = 1e-6
GQA_HEADS = 8
GQA_KV_HEADS = 2
GQA_HEAD_DIM = 64
MLA_HEADS = 8
MLA_Q_LORA = 256
MLA_KV_LORA = 128
MLA_NOPE = 64
MLA_ROPE = 32
MLA_V = 64
N_EXPERTS = 16
CAPACITY_FACTOR = 2
D_FF_EXPERT = 1024

LANES = 128
HALF = LANES // 2
D_IN_PAD = 1280
VMEM_LIMIT = 56 * 1024 * 1024
LOG2E = math.log2(math.e)

F32 = jnp.float32
BF16 = jnp.bfloat16


def _cparams(sem):
    return pltpu.CompilerParams(dimension_semantics=sem, vmem_limit_bytes=VMEM_LIMIT)


def _rope_tables(seq_len):
    rows = seq_len // GRID_W
    row = jnp.repeat(jnp.arange(rows, dtype=F32), GRID_W)
    col = jnp.tile(jnp.arange(GRID_W, dtype=F32), rows)

    def angles(rot_dim):
        half = rot_dim // 2
        inv = ROPE_THETA ** (-jnp.arange(0, half, 2, dtype=F32) / half)
        ar = row[:, None] * inv[None, :]
        ac = col[:, None] * inv[None, :]
        return jnp.concatenate([ar, ar, ac, ac], axis=-1)

    lane = jnp.arange(LANES)
    a64 = angles(GQA_HEAD_DIM)
    cos_a = jnp.tile(jnp.cos(a64), (1, 2))
    sin_a = jnp.tile(jnp.sin(a64), (1, 2))
    low_a = (lane % 32) < 16
    up_a = jnp.where(low_a, -sin_a, 0.0)
    dn_a = jnp.where(low_a, 0.0, sin_a)

    a32 = angles(MLA_ROPE)
    pad = ((0, 0), (MLA_NOPE, LANES - MLA_NOPE - MLA_ROPE))
    cos_b = jnp.pad(jnp.cos(a32) - 1.0, pad) + 1.0
    sin_b = jnp.pad(jnp.sin(a32), pad)
    low_b = ((lane - MLA_NOPE) % 16) < 8
    up_b = jnp.where(low_b, -sin_b, 0.0)
    dn_b = jnp.where(low_b, 0.0, sin_b)
    return jnp.stack([cos_a, up_a, dn_a, cos_b, up_b, dn_b])


def _segment_mean_matrix(groups):
    m = jnp.zeros((LANES, LANES), F32)
    for start, width in groups:
        m = m.at[start:start + width, start:start + width].set(1.0 / width)
    return m.astype(BF16)


def _prep_weights(p):
    w_in = p['w_in']
    kpe_cols = jnp.pad(w_in[:, 1152:1184], ((0, 0), (MLA_NOPE, LANES - MLA_NOPE - MLA_ROPE)))
    w_in_p = jnp.concatenate([w_in[:, :1152], kpe_cols], axis=1).astype(BF16)
    w_uq = p['mla_w_uq'].reshape(MLA_Q_LORA, MLA_HEADS, MLA_NOPE + MLA_ROPE)
    w_uq_p = jnp.pad(w_uq, ((0, 0), (0, 0), (0, LANES - MLA_NOPE - MLA_ROPE)))
    w_uq_p = w_uq_p.reshape(MLA_Q_LORA, MLA_HEADS * LANES).astype(BF16)
    w_ukv = p['mla_w_ukv'].reshape(MLA_KV_LORA, MLA_HEADS, MLA_NOPE + MLA_V)
    w_uk_p = jnp.pad(w_ukv[:, :, :MLA_NOPE], ((0, 0), (0, 0), (0, LANES - MLA_NOPE)))
    w_uk_p = w_uk_p.reshape(MLA_KV_LORA, MLA_HEADS * LANES).astype(BF16)
    w_uv = w_ukv[:, :, MLA_NOPE:].reshape(MLA_KV_LORA, MLA_HEADS * MLA_V).astype(BF16)

    def row(v):
        return v.reshape(1, -1).astype(F32)

    zeros32 = jnp.zeros((LANES - MLA_NOPE - MLA_ROPE,), F32)
    zeros64 = jnp.zeros((HALF,), F32)
    return dict(
        w_in=w_in_p, w_uq=w_uq_p, w_uk=w_uk_p, w_uv=w_uv,
        g_attn=row(p['norm_attn']),
        g_q=row(jnp.tile(p['gqa_q_norm'], 2)),
        g_k=row(jnp.tile(p['gqa_k_norm'], 2)),
        g_cq=row(p['mla_q_a_norm']),
        g_qm=row(jnp.concatenate([p['mla_q_nope_norm'], p['mla_q_pe_norm'], zeros32])),
        g_ckv=row(p['mla_kv_a_norm']),
        g_kn=row(jnp.concatenate([p['mla_k_nope_norm'], zeros64])),
        g_kpe=row(jnp.concatenate([zeros64, p['mla_k_pe_norm'], zeros32])),
        s_gqa=_segment_mean_matrix([(0, HALF), (HALF, HALF)]),
        s_mla=_segment_mean_matrix([(0, MLA_NOPE), (MLA_NOPE, MLA_ROPE)]),
        g_oa=row(p['out_norm_a']), g_ob=row(p['out_norm_b']),
        w_o=p['w_o'].astype(BF16),
        g_ffn=row(p['norm_ffn']),
        w_r=jnp.pad(p['w_router'], ((0, 0), (0, LANES - N_EXPERTS))).astype(BF16),
        w_gate=p['w_gate'].astype(BF16), w_up=p['w_up'].astype(BF16),
        w_down=p['w_down'].astype(BF16),
    )


def _rms(x, gain):
    ms = jnp.mean(x * x, axis=-1, keepdims=True)
    return x * lax.rsqrt(ms + RMS_EPS) * gain


def _segment_rms(x, seg_mat, gain):
    ms = jnp.dot((x * x).astype(BF16), seg_mat, preferred_element_type=F32)
    return x * lax.rsqrt(ms + RMS_EPS) * gain


def _rope(x, cos, up, dn, quarter):
    return (x * cos + pltpu.roll(x, LANES - quarter, 1) * up + pltpu.roll(x, quarter, 1) * dn)


def _pre_attn_kernel(x_ref, tab_ref, g_attn, w_in, g_q, g_k, g_cq, w_uq, g_qm, g_ckv, w_uk, w_uv,
                     g_kn, g_kpe, s_gqa, s_mla,
                     q_ref, k_ref, v_ref, qm_ref, km_ref, vm_ref):
    cos_a, up_a, dn_a = tab_ref[0], tab_ref[1], tab_ref[2]
    cos_b, up_b, dn_b = tab_ref[3], tab_ref[4], tab_ref[5]
    hn = _rms(x_ref[...], g_attn[...]).astype(BF16)
    proj = jnp.dot(hn, w_in[...], preferred_element_type=F32)

    sg = s_gqa[...]
    sm = s_mla[...]
    q_scale = LOG2E / math.sqrt(GQA_HEAD_DIM)
    for c in range(GQA_HEADS // 2):
        xc = _segment_rms(proj[:, c * LANES:(c + 1) * LANES], sg, g_q[...])
        q_ref[c] = (_rope(xc, cos_a, up_a, dn_a, 16) * q_scale).astype(BF16)

    low = lax.broadcasted_iota(jnp.int32, (1, LANES), 1) < HALF
    k01 = _rope(_segment_rms(proj[:, 512:640], sg, g_k[...]), cos_a, up_a, dn_a, 16)
    k10 = pltpu.roll(k01, HALF, 1)
    k_ref[0] = jnp.where(low, k01, 0.0).astype(BF16)
    k_ref[1] = jnp.where(low, 0.0, k10).astype(BF16)
    k_ref[2] = jnp.where(low, k10, 0.0).astype(BF16)
    k_ref[3] = jnp.where(low, 0.0, k01).astype(BF16)
    v01 = proj[:, 640:768]
    v_ref[0] = jnp.transpose(v01).astype(BF16)
    v_ref[1] = jnp.transpose(pltpu.roll(v01, HALF, 1)).astype(BF16)

    cq = _rms(proj[:, 768:1024], g_cq[...]).astype(BF16)
    qb = jnp.dot(cq, w_uq[...], preferred_element_type=F32)
    m_scale = LOG2E / math.sqrt(MLA_NOPE + MLA_ROPE)
    ckv = _rms(proj[:, 1024:1152], g_ckv[...]).astype(BF16)
    kn = jnp.dot(ckv, w_uk[...], preferred_element_type=F32)
    vv = jnp.dot(ckv, w_uv[...], preferred_element_type=F32)
    kpe = _rope(_segment_rms(proj[:, 1152:1280], sm, g_kpe[...]), cos_b, up_b, dn_b, 8)
    for h in range(MLA_HEADS):
        qh = _segment_rms(qb[:, h * LANES:(h + 1) * LANES], sm, g_qm[...])
        qm_ref[h] = (_rope(qh, cos_b, up_b, dn_b, 8) * m_scale).astype(BF16)
        kh = _segment_rms(kn[:, h * LANES:(h + 1) * LANES], sm, g_kn[...])
        km_ref[h] = (kh + kpe).astype(BF16)
    for c in range(MLA_HEADS // 2):
        vm_ref[c] = jnp.transpose(vv[:, c * LANES:(c + 1) * LANES]).astype(BF16)


def _full(shape):
    return pl.BlockSpec(shape, lambda *_: (0,) * len(shape))


def _pre_attention(x, tables, w, tm):
    b, s, d = x.shape
    nt = s // tm
    grid = (b, nt)

    def head_major(n):
        return pl.BlockSpec((None, n, tm, LANES), lambda bi, i: (bi, 0, i, 0))

    weights = [w['g_attn'], w['w_in'], w['g_q'], w['g_k'], w['g_cq'], w['w_uq'], w['g_qm'],
               w['g_ckv'], w['w_uk'], w['w_uv'], w['g_kn'], w['g_kpe'], w['s_gqa'], w['s_mla']]
    in_specs = [pl.BlockSpec((None, tm, d), lambda bi, i: (bi, i, 0)),
                pl.BlockSpec((6, tm, LANES), lambda bi, i: (0, i, 0))]
    in_specs += [_full(a.shape) for a in weights]
    def head_major_t(n):
        return pl.BlockSpec((None, n, None, LANES, tm), lambda bi, i: (bi, 0, i, 0, 0))

    counts = (4, 4, 2, 8, 8, 4)
    transposed = (False, False, True, False, False, True)
    out_shape = [jax.ShapeDtypeStruct((b, n, nt, LANES, tm) if t else (b, n, s, LANES), BF16)
                 for n, t in zip(counts, transposed)]
    out_specs = [head_major_t(n) if t else head_major(n) for n, t in zip(counts, transposed)]
    return pl.pallas_call(
        _pre_attn_kernel, grid=grid, in_specs=in_specs, out_specs=out_specs, out_shape=out_shape,
        compiler_params=_cparams(("parallel", "parallel")),
    )(x, tables, *weights)


def _attn_kernel(qe_ref, qo_ref, ke_ref, ko_ref, ve_ref, vo_ref, o_ref, m_sc, l_sc, acc_sc, *, tk):
    n_chunks = ve_ref.shape[0]
    refs = ((qe_ref, ke_ref, ve_ref), (qo_ref, ko_ref, vo_ref))
    qts = [jnp.transpose(q_ref[...].astype(F32)).astype(BF16) for q_ref, _, _ in refs]
    m_sc[...] = jnp.full(m_sc.shape, -jnp.inf, F32)
    l_sc[...] = jnp.zeros(l_sc.shape, F32)
    acc_sc[...] = jnp.zeros(acc_sc.shape, F32)

    def body(j, carry):
        start = pl.multiple_of(j * tk, tk)
        for h, (_, k_ref, v_ref) in enumerate(refs):
            st = jnp.dot(k_ref[pl.ds(start, tk), :], qts[h], preferred_element_type=F32)
            m_prev = m_sc[h]
            m_new = jnp.maximum(m_prev, jnp.max(st, axis=0, keepdims=True))
            alpha = jnp.exp2(m_prev - m_new)
            p = jnp.exp2(st - m_new)
            l_sc[h] = alpha * l_sc[h] + jnp.sum(p, axis=0, keepdims=True)
            acc_sc[h] = alpha * acc_sc[h] + jnp.dot(v_ref[j], p.astype(BF16),
                                                    preferred_element_type=F32)
            m_sc[h] = m_new
        return carry

    lax.fori_loop(0, n_chunks, body, 0)
    ot = jnp.concatenate([(acc_sc[0] / l_sc[0])[:HALF], (acc_sc[1] / l_sc[1])[HALF:]], axis=0)
    o_ref[...] = jnp.transpose(ot)


def _attention(q_arr, k_arr, vt_arr, maps, tq):
    b, _, s, _ = q_arr.shape
    n_pairs = 4
    n_chunks, _, tk = vt_arr.shape[2:]
    grid = (b, n_pairs, s // tq)
    qe, qo, ke, ko, ve, vo = maps

    def qspec(f):
        return pl.BlockSpec((None, None, tq, LANES), lambda bi, p, i: (bi, f(p), i, 0))

    def kspec(f):
        return pl.BlockSpec((None, None, s, LANES), lambda bi, p, i: (bi, f(p), 0, 0))

    def vspec(f):
        return pl.BlockSpec((None, None, n_chunks, LANES, tk),
                            lambda bi, p, i: (bi, f(p), 0, 0, 0))

    return pl.pallas_call(
        functools.partial(_attn_kernel, tk=tk), grid=grid,
        in_specs=[qspec(qe), qspec(qo), kspec(ke), kspec(ko), vspec(ve), vspec(vo)],
        out_specs=pl.BlockSpec((None, tq, LANES), lambda bi, p, i: (bi, i, p)),
        out_shape=jax.ShapeDtypeStruct((b, s, n_pairs * LANES), F32),
        scratch_shapes=[pltpu.VMEM((2, 1, tq), F32), pltpu.VMEM((2, 1, tq), F32),
                        pltpu.VMEM((2, LANES, tq), F32)],
        compiler_params=_cparams(("parallel", "parallel", "parallel")),
    )(q_arr, q_arr, k_arr, k_arr, vt_arr, vt_arr)


_GQA_MAPS = (lambda p: p, lambda p: p,
             lambda p: 2 * (p // 2), lambda p: 2 * (p // 2) + 1,
             lambda p: p // 2, lambda p: 1 - p // 2)
_MLA_MAPS = (lambda p: 2 * p, lambda p: 2 * p + 1,
             lambda p: 2 * p, lambda p: 2 * p + 1,
             lambda p: p, lambda p: p)


def _post_attn_kernel(x_ref, oa_ref, ob_ref, g_oa, g_ob, w_o, g_ffn, w_r,
                      x1_ref, h2_ref, afft_ref, aff_ref):
    na = _rms(oa_ref[...], g_oa[...]).astype(BF16)
    nb = _rms(ob_ref[...], g_ob[...]).astype(BF16)
    half = na.shape[1]
    mix = (jnp.dot(na, w_o[:half, :], preferred_element_type=F32)
           + jnp.dot(nb, w_o[half:, :], preferred_element_type=F32))
    x1 = x_ref[...] + mix
    x1_ref[...] = x1
    h2 = _rms(x1, g_ffn[...]).astype(BF16)
    h2_ref[...] = h2
    logits = jnp.dot(h2, w_r[...], preferred_element_type=F32)
    valid = lax.broadcasted_iota(jnp.int32, (1, LANES), 1) < N_EXPERTS
    logits = jnp.where(valid, logits, -jnp.inf)
    e = jnp.exp(logits - jnp.max(logits, axis=-1, keepdims=True))
    aff = e / jnp.sum(e, axis=-1, keepdims=True)
    afft_ref[...] = jnp.transpose(aff)[:N_EXPERTS, :]
    aff_ref[...] = aff[:, :N_EXPERTS]


def _post_attention(x, oa, ob, w, tm):
    b, s, d = x.shape
    nt = s // tm
    tok = lambda width: pl.BlockSpec((None, tm, width), lambda bi, i: (bi, i, 0))
    weights = [w['g_oa'], w['g_ob'], w['w_o'], w['g_ffn'], w['w_r']]
    return pl.pallas_call(
        _post_attn_kernel, grid=(b, nt),
        in_specs=[tok(d), tok(oa.shape[2]), tok(ob.shape[2])] + [_full(a.shape) for a in weights],
        out_specs=[tok(d), tok(d),
                   pl.BlockSpec((N_EXPERTS, tm), lambda bi, i: (0, bi * nt + i)),
                   tok(N_EXPERTS)],
        out_shape=[jax.ShapeDtypeStruct((b, s, d), F32), jax.ShapeDtypeStruct((b, s, d), BF16),
                   jax.ShapeDtypeStruct((N_EXPERTS, b * s), F32),
                   jax.ShapeDtypeStruct((b, s, N_EXPERTS), F32)],
        compiler_params=_cparams(("parallel", "parallel")),
    )(x, oa, ob, *weights)


SLOT_CHUNK = LANES


def _threshold_kernel(aff_ref, thr_ref, need_ref, *, cap):
    def count(mask):
        return jnp.sum(jnp.where(mask, 1.0, 0.0), axis=1, keepdims=True)

    def body(i, prefix):
        cand = prefix | jnp.left_shift(jnp.int32(1), 30 - i)
        cnt = count(pltpu.bitcast(aff_ref[...], jnp.int32) >= cand)
        return jnp.where(cnt >= cap, cand, prefix)

    thr = lax.fori_loop(0, 31, body, jnp.zeros((N_EXPERTS, 1), jnp.int32))
    greater = count(pltpu.bitcast(aff_ref[...], jnp.int32) > thr)
    thr_ref[...] = thr
    need_ref[...] = (cap - greater).astype(jnp.int32)


def _position_kernel(thr_ref, need_ref, aff_ref, upper_ref, lower_ref, pos_ref):
    e = pl.program_id(0)
    bits = pltpu.bitcast(aff_ref[...], jnp.int32)
    thr = thr_ref[e]
    need = need_ref[e].astype(F32)
    upper = upper_ref[...]
    lower = lower_ref[...]

    def inclusive_cumsum(mask):
        m = jnp.where(mask, 1.0, 0.0).astype(BF16)
        within = jnp.dot(m, upper, preferred_element_type=F32)
        above = jnp.dot(lower, m, preferred_element_type=F32)
        return within + jnp.sum(above, axis=1, keepdims=True)

    greater = bits > thr
    equal = bits == thr
    equal_rank = inclusive_cumsum(equal) - jnp.where(equal, 1.0, 0.0)
    selected = greater | (equal & (equal_rank < need))
    pos = inclusive_cumsum(selected) - 1.0
    pos_ref[...] = jnp.where(selected, pos, -1.0).astype(jnp.int32)


def _select(afft, cap):
    e, n = afft.shape
    nb = n // LANES
    thr, need = pl.pallas_call(
        functools.partial(_threshold_kernel, cap=cap),
        out_shape=[jax.ShapeDtypeStruct((e, 1), jnp.int32)] * 2,
        compiler_params=pltpu.CompilerParams(vmem_limit_bytes=VMEM_LIMIT),
    )(afft)
    upper = jnp.triu(jnp.ones((LANES, LANES), F32)).astype(BF16)
    lower = jnp.tril(jnp.ones((nb, nb), F32), -1).astype(BF16)
    pos = pl.pallas_call(
        _position_kernel,
        grid_spec=pltpu.PrefetchScalarGridSpec(
            num_scalar_prefetch=2, grid=(e,),
            in_specs=[pl.BlockSpec((nb, LANES), lambda ei, *_: (ei, 0)),
                      pl.BlockSpec((LANES, LANES), lambda ei, *_: (0, 0)),
                      pl.BlockSpec((nb, nb), lambda ei, *_: (0, 0))],
            out_specs=pl.BlockSpec((nb, LANES), lambda ei, *_: (ei, 0))),
        out_shape=jax.ShapeDtypeStruct((e * nb, LANES), jnp.int32),
        compiler_params=_cparams(("parallel",)),
    )(thr.reshape(e), need.reshape(e), afft.reshape(e * nb, LANES), upper, lower)
    return pos.reshape(e, n)


def _tile_starts(pos, tile):
    e, n = pos.shape
    cnt = jnp.sum((pos >= 0).reshape(e, n // tile, tile), axis=-1, dtype=jnp.int32)
    return jnp.concatenate([jnp.zeros((e, 1), jnp.int32), jnp.cumsum(cnt, axis=1)], axis=1)


def _dispatch_kernel(even_ref, odd_ref, st_ref, pos_ref, h_ref, xe_even, xe_odd, *, nt, bs):
    e = pl.program_id(0)
    t = pl.program_id(1)
    i = e * nt + t
    prev = jnp.maximum(i - 1, 0)

    @pl.when((t == 0) | (even_ref[i] != even_ref[prev]))
    def _():
        xe_even[...] = jnp.zeros(xe_even.shape, xe_even.dtype)

    @pl.when((t == 0) | (odd_ref[i] != odd_ref[prev]))
    def _():
        xe_odd[...] = jnp.zeros(xe_odd.shape, xe_odd.dtype)

    s0 = st_ref[e * (nt + 1) + t]
    s1 = st_ref[e * (nt + 1) + t + 1]
    chunks_per_block = bs // SLOT_CHUNK

    def body(g, carry):
        slot = g * SLOT_CHUNK + lax.broadcasted_iota(jnp.int32, (SLOT_CHUNK, 1), 0)
        onehot = jnp.where(pos_ref[...] == slot, 1.0, 0.0).astype(BF16)
        val = jnp.dot(onehot, h_ref[...], preferred_element_type=F32).astype(xe_even.dtype)
        block = g // chunks_per_block
        off = pl.multiple_of((g % chunks_per_block) * SLOT_CHUNK, SLOT_CHUNK)

        @pl.when(block % 2 == 0)
        def _():
            xe_even[pl.ds(off, SLOT_CHUNK), :] += val

        @pl.when(block % 2 == 1)
        def _():
            xe_odd[pl.ds(off, SLOT_CHUNK), :] += val

        return carry

    lax.fori_loop(s0 // SLOT_CHUNK, (s1 + SLOT_CHUNK - 1) // SLOT_CHUNK, body, 0)


def _dispatch(h2, pos, cap, bs):
    n, d = h2.shape
    e = pos.shape[0]
    tile = bs
    nt = n // tile
    nb = cap // bs
    assert cap % (2 * bs) == 0 and n % tile == 0
    starts = _tile_starts(pos, tile)
    b0 = starts[:, :-1] // bs
    even_idx = ((b0 + 1) // 2).reshape(-1)
    odd_idx = (b0 // 2).reshape(-1)
    return pl.pallas_call(
        functools.partial(_dispatch_kernel, nt=nt, bs=bs),
        grid_spec=pltpu.PrefetchScalarGridSpec(
            num_scalar_prefetch=3, grid=(e, nt),
            in_specs=[pl.BlockSpec((None, 1, tile), lambda ei, t, *_: (ei, 0, t)),
                      pl.BlockSpec((tile, d), lambda ei, t, *_: (t, 0))],
            out_specs=[pl.BlockSpec((None, None, bs, d),
                                    lambda ei, t, ev, od, st: (ei, ev[ei * nt + t], 0, 0)),
                       pl.BlockSpec((None, None, bs, d),
                                    lambda ei, t, ev, od, st: (ei, od[ei * nt + t], 0, 0))]),
        out_shape=[jax.ShapeDtypeStruct((e, nb // 2 + 1, bs, d), BF16),
                   jax.ShapeDtypeStruct((e, nb // 2, bs, d), BF16)],
        compiler_params=_cparams(("arbitrary", "arbitrary")),
    )(even_idx, odd_idx, starts.reshape(-1), pos.reshape(e, 1, n), h2)


def _combine_kernel(st_ref, x1_ref, pos_ref, aff_ref, *refs, nt, by):
    y_ref = refs[-1]
    t = pl.program_id(0)
    y_ref[...] = x1_ref[...]
    for e in range(N_EXPERTS):
        s0 = st_ref[e * (nt + 1) + t]
        s1 = st_ref[e * (nt + 1) + t + 1]
        b0 = s0 // by
        for j in range(2):
            lo = (b0 + (j + b0) % 2) * by

            @pl.when((s1 > s0) & (s1 > lo) & (s0 < lo + by))
            def _(e=e, j=j, lo=lo):
                slot = lo + lax.broadcasted_iota(jnp.int32, (1, by), 1)
                onehot = jnp.where(pos_ref[:, e:e + 1] == slot, 1.0, 0.0).astype(BF16)
                part = jnp.dot(onehot, refs[2 * e + j][...], preferred_element_type=F32)
                y_ref[...] += aff_ref[:, e:e + 1] * part


def _combine(x1, pos_tok, aff_tok, ye, starts, tile):
    n, d = x1.shape
    e, cap, _ = ye.shape
    by = tile
    nt = n // tile
    nby = cap // by

    def ye_spec(ei, j):
        def index(t, st):
            b0 = st[ei * (nt + 1) + t] // by
            return (ei, jnp.minimum(b0 + (j + b0) % 2, nby - 1), 0)
        return pl.BlockSpec((None, by, d), index)

    tok = lambda width: pl.BlockSpec((tile, width), lambda t, st: (t, 0))
    ye_specs = [ye_spec(ei, j) for ei in range(e) for j in range(2)]
    return pl.pallas_call(
        functools.partial(_combine_kernel, nt=nt, by=by),
        grid_spec=pltpu.PrefetchScalarGridSpec(
            num_scalar_prefetch=1, grid=(nt,),
            in_specs=[tok(d), tok(e), tok(e)] + ye_specs,
            out_specs=tok(d)),
        out_shape=jax.ShapeDtypeStruct((n, d), F32),
        compiler_params=_cparams(("arbitrary",)),
    )(starts.reshape(-1), x1, pos_tok, aff_tok, *([ye] * (2 * e)))


def _expert_kernel(xe_even, xe_odd, wg_ref, wu_ref, wd_ref, ye_ref):
    xe = jnp.where(pl.program_id(1) % 2 == 0, xe_even[...], xe_odd[...])
    a = jnp.dot(xe, wg_ref[...], preferred_element_type=F32)
    u = jnp.dot(xe, wu_ref[...], preferred_element_type=F32)
    hid = (a * jax.nn.sigmoid(a) * u).astype(BF16)
    ye_ref[...] = jnp.dot(hid, wd_ref[...], preferred_element_type=F32).astype(ye_ref.dtype)


def _expert_ffn(xe_even, xe_odd, w):
    e, nbh, bs, d = xe_odd.shape
    f = w['w_gate'].shape[2]
    xspec = pl.BlockSpec((None, None, bs, d), lambda ei, ci: (ei, ci // 2, 0, 0))
    wspec = lambda r, cc: pl.BlockSpec((None, r, cc), lambda ei, ci: (ei, 0, 0))
    return pl.pallas_call(
        _expert_kernel, grid=(e, 2 * nbh),
        in_specs=[xspec, xspec, wspec(d, f), wspec(d, f), wspec(f, d)],
        out_specs=pl.BlockSpec((None, bs, d), lambda ei, ci: (ei, ci, 0)),
        out_shape=jax.ShapeDtypeStruct((e, 2 * nbh * bs, d), BF16),
        compiler_params=_cparams(("parallel", "parallel")),
    )(xe_even, xe_odd, w['w_gate'], w['w_up'], w['w_down'])


def _tile(n, pref):
    t = min(n, pref)
    assert n % t == 0
    return t


def _encoder_layer(x, w):
    b, s, d = x.shape
    n = b * s
    tk = _tile(s, 1024)
    tables = _rope_tables(s)
    q, k, vt, qm, km, vmt = _pre_attention(x, tables, w, tk)
    tq = _tile(s, 1024)
    oa = _attention(q, k, vt, _GQA_MAPS, tq)
    ob = _attention(qm, km, vmt, _MLA_MAPS, tq)
    x1, h2, afft, aff_tok = _post_attention(x, oa, ob, w, _tile(s, 512))

    cap = CAPACITY_FACTOR * n // N_EXPERTS
    pos = _select(afft, cap)
    xe_even, xe_odd = _dispatch(h2.reshape(n, d), pos, cap, _tile(cap // 2, 1024))
    ye = _expert_ffn(xe_even, xe_odd, w)
    ctile = _tile(cap // 2, 256)
    y = _combine(x1.reshape(n, d), pos.T, aff_tok.reshape(n, N_EXPERTS), ye,
                 _tile_starts(pos, ctile), ctile)
    return y.reshape(b, s, d)


def kernel(x_prompt, x_sample, norm_attn, w_in, gqa_q_norm, gqa_k_norm, mla_q_a_norm, mla_w_uq,
           mla_q_nope_norm, mla_q_pe_norm, mla_kv_a_norm, mla_w_ukv, mla_k_nope_norm,
           mla_k_pe_norm, out_norm_a, out_norm_b, w_o, norm_ffn, w_router, w_gate, w_up, w_down):
    p = dict(norm_attn=norm_attn, w_in=w_in, gqa_q_norm=gqa_q_norm, gqa_k_norm=gqa_k_norm,
             mla_q_a_norm=mla_q_a_norm, mla_w_uq=mla_w_uq, mla_q_nope_norm=mla_q_nope_norm,
             mla_q_pe_norm=mla_q_pe_norm, mla_kv_a_norm=mla_kv_a_norm, mla_w_ukv=mla_w_ukv,
             mla_k_nope_norm=mla_k_nope_norm, mla_k_pe_norm=mla_k_pe_norm,
             out_norm_a=out_norm_a, out_norm_b=out_norm_b, w_o=w_o, norm_ffn=norm_ffn,
             w_router=w_router, w_gate=w_gate, w_up=w_up, w_down=w_down)
    w = _prep_weights(p)
    return (_encoder_layer(x_prompt, w), _encoder_layer(x_sample, w))
```

```python
import functools
import math

import jax
import jax.numpy as jnp
from jax import lax
from jax.experimental import pallas as pl
from jax.experimental.pallas import tpu as pltpu

D_MODEL = 1024
GRID_W = 64
ROPE_THETA = 10000.0
RMS_EPS = 1e-6
GQA_HEADS = 8
GQA_KV_HEADS = 2
GQA_HEAD_DIM = 64
MLA_HEADS = 8
MLA_Q_LORA = 256
MLA_KV_LORA = 128
MLA_NOPE = 64
MLA_ROPE = 32
MLA_V = 64
N_EXPERTS = 16
CAPACITY_FACTOR = 2
D_FF_EXPERT = 1024

LANES = 128
HALF = LANES // 2
D_IN_PAD = 1280
VMEM_LIMIT = 56 * 1024 * 1024
LOG2E = math.log2(math.e)

F32 = jnp.float32
BF16 = jnp.bfloat16


def _cparams(sem):
    return pltpu.CompilerParams(dimension_semantics=sem, vmem_limit_bytes=VMEM_LIMIT)


def _rope_tables(seq_len):
    rows = seq_len // GRID_W
    row = jnp.repeat(jnp.arange(rows, dtype=F32), GRID_W)
    col = jnp.tile(jnp.arange(GRID_W, dtype=F32), rows)

    def angles(rot_dim):
        half = rot_dim // 2
        inv = ROPE_THETA ** (-jnp.arange(0, half, 2, dtype=F32) / half)
        ar = row[:, None] * inv[None, :]
        ac = col[:, None] * inv[None, :]
        return jnp.concatenate([ar, ar, ac, ac], axis=-1)

    lane = jnp.arange(LANES)
    a64 = angles(GQA_HEAD_DIM)
    cos_a = jnp.tile(jnp.cos(a64), (1, 2))
    sin_a = jnp.tile(jnp.sin(a64), (1, 2))
    low_a = (lane % 32) < 16
    up_a = jnp.where(low_a, -sin_a, 0.0)
    dn_a = jnp.where(low_a, 0.0, sin_a)

    a32 = angles(MLA_ROPE)
    pad = ((0, 0), (MLA_NOPE, LANES - MLA_NOPE - MLA_ROPE))
    cos_b = jnp.pad(jnp.cos(a32) - 1.0, pad) + 1.0
    sin_b = jnp.pad(jnp.sin(a32), pad)
    low_b = ((lane - MLA_NOPE) % 16) < 8
    up_b = jnp.where(low_b, -sin_b, 0.0)
    dn_b = jnp.where(low_b, 0.0, sin_b)
    return jnp.stack([cos_a, up_a, dn_a, cos_b, up_b, dn_b])


def _segment_mean_matrix(groups):
    m = jnp.zeros((LANES, LANES), F32)
    for start, width in groups:
        m = m.at[start:start + width, start:start + width].set(1.0 / width)
    return m.astype(BF16)


def _prep_weights(p):
    w_in = p['w_in']
    kpe_cols = jnp.pad(w_in[:, 1152:1184], ((0, 0), (MLA_NOPE, LANES - MLA_NOPE - MLA_ROPE)))
    w_in_p = jnp.concatenate([w_in[:, :1152], kpe_cols], axis=1).astype(BF16)
    w_uq = p['mla_w_uq'].reshape(MLA_Q_LORA, MLA_HEADS, MLA_NOPE + MLA_ROPE)
    w_uq_p = jnp.pad(w_uq, ((0, 0), (0, 0), (0, LANES - MLA_NOPE - MLA_ROPE)))
    w_uq_p = w_uq_p.reshape(MLA_Q_LORA, MLA_HEADS * LANES).astype(BF16)
    w_ukv = p['mla_w_ukv'].reshape(MLA_KV_LORA, MLA_HEADS, MLA_NOPE + MLA_V)
    w_uk_p = jnp.pad(w_ukv[:, :, :MLA_NOPE], ((0, 0), (0, 0), (0, LANES - MLA_NOPE)))
    w_uk_p = w_uk_p.reshape(MLA_KV_LORA, MLA_HEADS * LANES).astype(BF16)
    w_uv = w_ukv[:, :, MLA_NOPE:].reshape(MLA_KV_LORA, MLA_HEADS * MLA_V).astype(BF16)

    def row(v):
        return v.reshape(1, -1).astype(F32)

    zeros32 = jnp.zeros((LANES - MLA_NOPE - MLA_ROPE,), F32)
    zeros64 = jnp.zeros((HALF,), F32)
    return dict(
        w_in=w_in_p, w_uq=w_uq_p, w_uk=w_uk_p, w_uv=w_uv,
        g_attn=row(p['norm_attn']),
        g_q=row(jnp.tile(p['gqa_q_norm'], 2)),
        g_k=row(jnp.tile(p['gqa_k_norm'], 2)),
        g_cq=row(p['mla_q_a_norm']),
        g_qm=row(jnp.concatenate([p['mla_q_nope_norm'], p['mla_q_pe_norm'], zeros32])),
        g_ckv=row(p['mla_kv_a_norm']),
        g_kn=row(jnp.concatenate([p['mla_k_nope_norm'], zeros64])),
        g_kpe=row(jnp.concatenate([zeros64, p['mla_k_pe_norm'], zeros32])),
        s_gqa=_segment_mean_matrix([(0, HALF), (HALF, HALF)]),
        s_mla=_segment_mean_matrix([(0, MLA_NOPE), (MLA_NOPE, MLA_ROPE)]),
        g_oa=row(p['out_norm_a']), g_ob=row(p['out_norm_b']),
        w_o=p['w_o'].astype(BF16),
        g_ffn=row(p['norm_ffn']),
        w_r=jnp.pad(p['w_router'], ((0, 0), (0, LANES - N_EXPERTS))).astype(BF16),
        w_gate=p['w_gate'].astype(BF16), w_up=p['w_up'].astype(BF16),
        w_down=p['w_down'].astype(BF16),
    )


def _rms(x, gain):
    ms = jnp.mean(x * x, axis=-1, keepdims=True)
    return x * lax.rsqrt(ms + RMS_EPS) * gain


def _segment_rms(x, seg_mat, gain):
    ms = jnp.dot((x * x).astype(BF16), seg_mat, preferred_element_type=F32)
    return x * lax.rsqrt(ms + RMS_EPS) * gain


def _rope(x, cos, up, dn, quarter):
    return (x * cos + pltpu.roll(x, LANES - quarter, 1) * up + pltpu.roll(x, quarter, 1) * dn)


def _pre_attn_kernel(x_ref, tab_ref, g_attn, w_in, g_q, g_k, g_cq, w_uq, g_qm, g_ckv, w_uk, w_uv,
                     g_kn, g_kpe, s_gqa, s_mla,
                     q_ref, k_ref, v_ref, qm_ref, km_ref, vm_ref):
    cos_a, up_a, dn_a = tab_ref[0], tab_ref[1], tab_ref[2]
    cos_b, up_b, dn_b = tab_ref[3], tab_ref[4], tab_ref[5]
    hn = _rms(x_ref[...], g_attn[...]).astype(BF16)
    proj = jnp.dot(hn, w_in[...], preferred_element_type=F32)

    sg = s_gqa[...]
    sm = s_mla[...]
    q_scale = LOG2E / math.sqrt(GQA_HEAD_DIM)
    for c in range(GQA_HEADS // 2):
        xc = _segment_rms(proj[:, c * LANES:(c + 1) * LANES], sg, g_q[...])
        q_ref[c] = (_rope(xc, cos_a, up_a, dn_a, 16) * q_scale).astype(BF16)

    low = lax.broadcasted_iota(jnp.int32, (1, LANES), 1) < HALF
    k01 = _rope(_segment_rms(proj[:, 512:640], sg, g_k[...]), cos_a, up_a, dn_a, 16)
    k10 = pltpu.roll(k01, HALF, 1)
    k_ref[0] = jnp.where(low, k01, 0.0).astype(BF16)
    k_ref[1] = jnp.where(low, 0.0, k10).astype(BF16)
    k_ref[2] = jnp.where(low, k10, 0.0).astype(BF16)
    k_ref[3] = jnp.where(low, 0.0, k01).astype(BF16)
    v01 = proj[:, 640:768]
    v_ref[0] = jnp.transpose(v01).astype(BF16)
    v_ref[1] = jnp.transpose(pltpu.roll(v01, HALF, 1)).astype(BF16)

    cq = _rms(proj[:, 768:1024], g_cq[...]).astype(BF16)
    qb = jnp.dot(cq, w_uq[...], preferred_element_type=F32)
    m_scale = LOG2E / math.sqrt(MLA_NOPE + MLA_ROPE)
    ckv = _rms(proj[:, 1024:1152], g_ckv[...]).astype(BF16)
    kn = jnp.dot(ckv, w_uk[...], preferred_element_type=F32)
    vv = jnp.dot(ckv, w_uv[...], preferred_element_type=F32)
    kpe = _rope(_segment_rms(proj[:, 1152:1280], sm, g_kpe[...]), cos_b, up_b, dn_b, 8)
    for h in range(MLA_HEADS):
        qh = _segment_rms(qb[:, h * LANES:(h + 1) * LANES], sm, g_qm[...])
        qm_ref[h] = (_rope(qh, cos_b, up_b, dn_b, 8) * m_scale).astype(BF16)
        kh = _segment_rms(kn[:, h * LANES:(h + 1) * LANES], sm, g_kn[...])
        km_ref[h] = (kh + kpe).astype(BF16)
    for c in range(MLA_HEADS // 2):
        vm_ref[c] = jnp.transpose(vv[:, c * LANES:(c + 1) * LANES]).astype(BF16)


def _full(shape):
    return pl.BlockSpec(shape, lambda *_: (0,) * len(shape))


def _pre_attention(x, tables, w, tm):
    b, s, d = x.shape
    nt = s // tm
    grid = (b, nt)

    def head_major(n):
        return pl.BlockSpec((None, n, tm, LANES), lambda bi, i: (bi, 0, i, 0))

    weights = [w['g_attn'], w['w_in'], w['g_q'], w['g_k'], w['g_cq'], w['w_uq'], w['g_qm'],
               w['g_ckv'], w['w_uk'], w['w_uv'], w['g_kn'], w['g_kpe'], w['s_gqa'], w['s_mla']]
    in_specs = [pl.BlockSpec((None, tm, d), lambda bi, i: (bi, i, 0)),
                pl.BlockSpec((6, tm, LANES), lambda bi, i: (0, i, 0))]
    in_specs += [_full(a.shape) for a in weights]
    def head_major_t(n):
        return pl.BlockSpec((None, n, None, LANES, tm), lambda bi, i: (bi, 0, i, 0, 0))

    counts = (4, 4, 2, 8, 8, 4)
    transposed = (False, False, True, False, False, True)
    out_shape = [jax.ShapeDtypeStruct((b, n, nt, LANES, tm) if t else (b, n, s, LANES), BF16)
                 for n, t in zip(counts, transposed)]
    out_specs = [head_major_t(n) if t else head_major(n) for n, t in zip(counts, transposed)]
    return pl.pallas_call(
        _pre_attn_kernel, grid=grid, in_specs=in_specs, out_specs=out_specs, out_shape=out_shape,
        compiler_params=_cparams(("parallel", "parallel")),
    )(x, tables, *weights)


def _attn_kernel(qe_ref, qo_ref, ke_ref, ko_ref, ve_ref, vo_ref, o_ref, acc_sc, *, tk):
    n_chunks = ve_ref.shape[0]
    tq = acc_sc.shape[2]
    refs = ((qe_ref, ke_ref, ve_ref), (qo_ref, ko_ref, vo_ref))
    qts = [jnp.transpose(q_ref[...].astype(F32)).astype(BF16) for q_ref, _, _ in refs]
    acc_sc[...] = jnp.zeros(acc_sc.shape, F32)

    def body(j, carry):
        start = pl.multiple_of(j * tk, tk)
        sts = [jnp.dot(k_ref[pl.ds(start, tk), :], qts[h], preferred_element_type=F32)
               for h, (_, k_ref, _) in enumerate(refs)]
        stats = []
        for h, (_, _, v_ref) in enumerate(refs):
            m_prev, l_prev = carry[h]
            m_new = jnp.maximum(m_prev, jnp.max(sts[h], axis=0, keepdims=True))
            alpha = jnp.exp2(m_prev - m_new)
            p = jnp.exp2(sts[h] - m_new)
            l_new = alpha * l_prev + jnp.sum(p, axis=0, keepdims=True)
            acc_sc[h] = alpha * acc_sc[h] + jnp.dot(v_ref[j], p.astype(BF16),
                                                    preferred_element_type=F32)
            stats.append((m_new, l_new))
        return tuple(stats)

    init = tuple((jnp.full((1, tq), -jnp.inf, F32), jnp.zeros((1, tq), F32)) for _ in refs)
    (_, l_even), (_, l_odd) = lax.fori_loop(0, n_chunks, body, init)
    ot = jnp.concatenate([(acc_sc[0] / l_even)[:HALF], (acc_sc[1] / l_odd)[HALF:]], axis=0)
    o_ref[...] = jnp.transpose(ot)


def _attention(q_arr, k_arr, vt_arr, maps, tq):
    b, _, s, _ = q_arr.shape
    n_pairs = 4
    n_chunks, _, tk = vt_arr.shape[2:]
    grid = (b, n_pairs, s // tq)
    qe, qo, ke, ko, ve, vo = maps

    def qspec(f):
        return pl.BlockSpec((None, None, tq, LANES), lambda bi, p, i: (bi, f(p), i, 0))

    def kspec(f):
        return pl.BlockSpec((None, None, s, LANES), lambda bi, p, i: (bi, f(p), 0, 0))

    def vspec(f):
        return pl.BlockSpec((None, None, n_chunks, LANES, tk),
                            lambda bi, p, i: (bi, f(p), 0, 0, 0))

    return pl.pallas_call(
        functools.partial(_attn_kernel, tk=tk), grid=grid,
        in_specs=[qspec(qe), qspec(qo), kspec(ke), kspec(ko), vspec(ve), vspec(vo)],
        out_specs=pl.BlockSpec((None, tq, LANES), lambda bi, p, i: (bi, i, p)),
        out_shape=jax.ShapeDtypeStruct((b, s, n_pairs * LANES), F32),
        scratch_shapes=[pltpu.VMEM((2, LANES, tq), F32)],
        compiler_params=_cparams(("parallel", "parallel", "parallel")),
    )(q_arr, q_arr, k_arr, k_arr, vt_arr, vt_arr)


_GQA_MAPS = (lambda p: p, lambda p: p,
             lambda p: 2 * (p // 2), lambda p: 2 * (p // 2) + 1,
             lambda p: p // 2, lambda p: 1 - p // 2)
_MLA_MAPS = (lambda p: 2 * p, lambda p: 2 * p + 1,
             lambda p: 2 * p, lambda p: 2 * p + 1,
             lambda p: p, lambda p: p)


def _post_attn_kernel(x_ref, oa_ref, ob_ref, g_oa, g_ob, w_o, g_ffn, w_r,
                      x1_ref, h2_ref, afft_ref, aff_ref):
    na = _rms(oa_ref[...], g_oa[...]).astype(BF16)
    nb = _rms(ob_ref[...], g_ob[...]).astype(BF16)
    half = na.shape[1]
    mix = (jnp.dot(na, w_o[:half, :], preferred_element_type=F32)
           + jnp.dot(nb, w_o[half:, :], preferred_element_type=F32))
    x1 = x_ref[...] + mix
    x1_ref[...] = x1
    h2 = _rms(x1, g_ffn[...]).astype(BF16)
    h2_ref[...] = h2
    logits = jnp.dot(h2, w_r[...], preferred_element_type=F32)
    valid = lax.broadcasted_iota(jnp.int32, (1, LANES), 1) < N_EXPERTS
    logits = jnp.where(valid, logits, -jnp.inf)
    e = jnp.exp(logits - jnp.max(logits, axis=-1, keepdims=True))
    aff = e / jnp.sum(e, axis=-1, keepdims=True)
    afft_ref[...] = jnp.transpose(aff)[:N_EXPERTS, :]
    aff_ref[...] = aff[:, :N_EXPERTS]


def _post_attention(x, oa, ob, w, tm):
    b, s, d = x.shape
    nt = s // tm
    tok = lambda width: pl.BlockSpec((None, tm, width), lambda bi, i: (bi, i, 0))
    weights = [w['g_oa'], w['g_ob'], w['w_o'], w['g_ffn'], w['w_r']]
    return pl.pallas_call(
        _post_attn_kernel, grid=(b, nt),
        in_specs=[tok(d), tok(oa.shape[2]), tok(ob.shape[2])] + [_full(a.shape) for a in weights],
        out_specs=[tok(d), tok(d),
                   pl.BlockSpec((N_EXPERTS, tm), lambda bi, i: (0, bi * nt + i)),
                   tok(N_EXPERTS)],
        out_shape=[jax.ShapeDtypeStruct((b, s, d), F32), jax.ShapeDtypeStruct((b, s, d), BF16),
                   jax.ShapeDtypeStruct((N_EXPERTS, b * s), F32),
                   jax.ShapeDtypeStruct((b, s, N_EXPERTS), F32)],
        compiler_params=_cparams(("parallel", "parallel")),
    )(x, oa, ob, *weights)


SLOT_CHUNK = LANES


def _threshold_kernel(aff_ref, thr_ref, need_ref, *, cap):
    def count(mask):
        return jnp.sum(jnp.where(mask, 1.0, 0.0), axis=1, keepdims=True)

    def body(i, prefix):
        cand = prefix | jnp.left_shift(jnp.int32(1), 30 - i)
        cnt = count(pltpu.bitcast(aff_ref[...], jnp.int32) >= cand)
        return jnp.where(cnt >= cap, cand, prefix)

    thr = lax.fori_loop(0, 31, body, jnp.zeros((N_EXPERTS, 1), jnp.int32))
    greater = count(pltpu.bitcast(aff_ref[...], jnp.int32) > thr)
    thr_ref[...] = thr
    need_ref[...] = (cap - greater).astype(jnp.int32)


def _position_kernel(thr_ref, need_ref, aff_ref, upper_ref, lower_ref, pos_ref):
    e = pl.program_id(0)
    bits = pltpu.bitcast(aff_ref[...], jnp.int32)
    thr = thr_ref[e]
    need = need_ref[e].astype(F32)
    upper = upper_ref[...]
    lower = lower_ref[...]

    def inclusive_cumsum(mask):
        m = jnp.where(mask, 1.0, 0.0).astype(BF16)
        within = jnp.dot(m, upper, preferred_element_type=F32)
        above = jnp.dot(lower, m, preferred_element_type=F32)
        return within + jnp.sum(above, axis=1, keepdims=True)

    greater = bits > thr
    equal = bits == thr
    equal_rank = inclusive_cumsum(equal) - jnp.where(equal, 1.0, 0.0)
    selected = greater | (equal & (equal_rank < need))
    pos = inclusive_cumsum(selected) - 1.0
    pos_ref[...] = jnp.where(selected, pos, -1.0).astype(jnp.int32)


def _select(afft, cap):
    e, n = afft.shape
    nb = n // LANES
    thr, need = pl.pallas_call(
        functools.partial(_threshold_kernel, cap=cap),
        out_shape=[jax.ShapeDtypeStruct((e, 1), jnp.int32)] * 2,
        compiler_params=pltpu.CompilerParams(vmem_limit_bytes=VMEM_LIMIT),
    )(afft)
    upper = jnp.triu(jnp.ones((LANES, LANES), F32)).astype(BF16)
    lower = jnp.tril(jnp.ones((nb, nb), F32), -1).astype(BF16)
    pos = pl.pallas_call(
        _position_kernel,
        grid_spec=pltpu.PrefetchScalarGridSpec(
            num_scalar_prefetch=2, grid=(e,),
            in_specs=[pl.BlockSpec((nb, LANES), lambda ei, *_: (ei, 0)),
                      pl.BlockSpec((LANES, LANES), lambda ei, *_: (0, 0)),
                      pl.BlockSpec((nb, nb), lambda ei, *_: (0, 0))],
            out_specs=pl.BlockSpec((nb, LANES), lambda ei, *_: (ei, 0))),
        out_shape=jax.ShapeDtypeStruct((e * nb, LANES), jnp.int32),
        compiler_params=_cparams(("parallel",)),
    )(thr.reshape(e), need.reshape(e), afft.reshape(e * nb, LANES), upper, lower)
    return pos.reshape(e, n)


def _tile_starts(pos, tile):
    e, n = pos.shape
    cnt = jnp.sum((pos >= 0).reshape(e, n // tile, tile), axis=-1, dtype=jnp.int32)
    return jnp.concatenate([jnp.zeros((e, 1), jnp.int32), jnp.cumsum(cnt, axis=1)], axis=1)


def _dispatch_kernel(even_ref, odd_ref, st_ref, pos_ref, h_ref, xe_even, xe_odd, *, nt, bs):
    e = pl.program_id(0)
    t = pl.program_id(1)
    i = e * nt + t
    prev = jnp.maximum(i - 1, 0)

    @pl.when((t == 0) | (even_ref[i] != even_ref[prev]))
    def _():
        xe_even[...] = jnp.zeros(xe_even.shape, xe_even.dtype)

    @pl.when((t == 0) | (odd_ref[i] != odd_ref[prev]))
    def _():
        xe_odd[...] = jnp.zeros(xe_odd.shape, xe_odd.dtype)

    s0 = st_ref[e * (nt + 1) + t]
    s1 = st_ref[e * (nt + 1) + t + 1]
    chunks_per_block = bs // SLOT_CHUNK

    def body(g, carry):
        slot = g * SLOT_CHUNK + lax.broadcasted_iota(jnp.int32, (SLOT_CHUNK, 1), 0)
        onehot = jnp.where(pos_ref[...] == slot, 1.0, 0.0).astype(BF16)
        val = jnp.dot(onehot, h_ref[...], preferred_element_type=F32).astype(xe_even.dtype)
        block = g // chunks_per_block
        off = pl.multiple_of((g % chunks_per_block) * SLOT_CHUNK, SLOT_CHUNK)

        @pl.when(block % 2 == 0)
        def _():
            xe_even[pl.ds(off, SLOT_CHUNK), :] += val

        @pl.when(block % 2 == 1)
        def _():
            xe_odd[pl.ds(off, SLOT_CHUNK), :] += val

        return carry

    lax.fori_loop(s0 // SLOT_CHUNK, (s1 + SLOT_CHUNK - 1) // SLOT_CHUNK, body, 0)


def _dispatch(h2, pos, cap, bs):
    n, d = h2.shape
    e = pos.shape[0]
    tile = bs
    nt = n // tile
    nb = cap // bs
    assert cap % (2 * bs) == 0 and n % tile == 0
    starts = _tile_starts(pos, tile)
    b0 = starts[:, :-1] // bs
    even_idx = ((b0 + 1) // 2).reshape(-1)
    odd_idx = (b0 // 2).reshape(-1)
    return pl.pallas_call(
        functools.partial(_dispatch_kernel, nt=nt, bs=bs),
        grid_spec=pltpu.PrefetchScalarGridSpec(
            num_scalar_prefetch=3, grid=(e, nt),
            in_specs=[pl.BlockSpec((None, 1, tile), lambda ei, t, *_: (ei, 0, t)),
                      pl.BlockSpec((tile, d), lambda ei, t, *_: (t, 0))],
            out_specs=[pl.BlockSpec((None, None, bs, d),
                                    lambda ei, t, ev, od, st: (ei, ev[ei * nt + t], 0, 0)),
                       pl.BlockSpec((None, None, bs, d),
                                    lambda ei, t, ev, od, st: (ei, od[ei * nt + t], 0, 0))]),
        out_shape=[jax.ShapeDtypeStruct((e, nb // 2 + 1, bs, d), BF16),
                   jax.ShapeDtypeStruct((e, nb // 2, bs, d), BF16)],
        compiler_params=_cparams(("arbitrary", "arbitrary")),
    )(even_idx, odd_idx, starts.reshape(-1), pos.reshape(e, 1, n), h2)


def _combine_kernel(st_ref, x1_ref, pos_ref, aff_ref, *refs, nt, by):
    y_ref = refs[-1]
    t = pl.program_id(0)

    def contribution(e, j):
        lo = (st_ref[e * (nt + 1) + t] // by + j) * by
        slot = lo + lax.broadcasted_iota(jnp.int32, (1, by), 1)
        onehot = jnp.where(pos_ref[:, e:e + 1] == slot, 1.0, 0.0).astype(BF16)
        part = jnp.dot(onehot, refs[2 * e + j][...], preferred_element_type=F32)
        return aff_ref[:, e:e + 1] * part

    acc = x1_ref[...]
    for e in range(N_EXPERTS):
        acc = acc + contribution(e, 0)
    y_ref[...] = acc
    for e in range(N_EXPERTS):
        spills_over = st_ref[e * (nt + 1) + t + 1] > (st_ref[e * (nt + 1) + t] // by + 1) * by

        @pl.when(spills_over)
        def _(e=e):
            y_ref[...] += contribution(e, 1)


def _combine(x1, pos_tok, aff_tok, ye, starts, tile):
    n, d = x1.shape
    e, cap, _ = ye.shape
    by = tile
    nt = n // tile
    nby = cap // by

    def ye_spec(ei, j):
        def index(t, st):
            return (ei, jnp.minimum(st[ei * (nt + 1) + t] // by + j, nby - 1), 0)
        return pl.BlockSpec((None, by, d), index)

    tok = lambda width: pl.BlockSpec((tile, width), lambda t, st: (t, 0))
    ye_specs = [ye_spec(ei, j) for ei in range(e) for j in range(2)]
    return pl.pallas_call(
        functools.partial(_combine_kernel, nt=nt, by=by),
        grid_spec=pltpu.PrefetchScalarGridSpec(
            num_scalar_prefetch=1, grid=(nt,),
            in_specs=[tok(d), tok(e), tok(e)] + ye_specs,
            out_specs=tok(d)),
        out_shape=jax.ShapeDtypeStruct((n, d), F32),
        compiler_params=_cparams(("arbitrary",)),
    )(starts.reshape(-1), x1, pos_tok, aff_tok, *([ye] * (2 * e)))


def _expert_kernel(xe_even, xe_odd, wg_ref, wu_ref, wd_ref, ye_ref):
    xe = jnp.where(pl.program_id(1) % 2 == 0, xe_even[...], xe_odd[...])
    a = jnp.dot(xe, wg_ref[...], preferred_element_type=F32)
    u = jnp.dot(xe, wu_ref[...], preferred_element_type=F32)
    hid = (a * jax.nn.sigmoid(a) * u).astype(BF16)
    ye_ref[...] = jnp.dot(hid, wd_ref[...], preferred_element_type=F32).astype(ye_ref.dtype)


def _expert_ffn(xe_even, xe_odd, w):
    e, nbh, bs, d = xe_odd.shape
    f = w['w_gate'].shape[2]
    xspec = pl.BlockSpec((None, None, bs, d), lambda ei, ci: (ei, ci // 2, 0, 0))
    wspec = lambda r, cc: pl.BlockSpec((None, r, cc), lambda ei, ci: (ei, 0, 0))
    return pl.pallas_call(
        _expert_kernel, grid=(e, 2 * nbh),
        in_specs=[xspec, xspec, wspec(d, f), wspec(d, f), wspec(f, d)],
        out_specs=pl.BlockSpec((None, bs, d), lambda ei, ci: (ei, ci, 0)),
        out_shape=jax.ShapeDtypeStruct((e, 2 * nbh * bs, d), BF16),
        compiler_params=_cparams(("parallel", "parallel")),
    )(xe_even, xe_odd, w['w_gate'], w['w_up'], w['w_down'])


def _tile(n, pref):
    t = min(n, pref)
    assert n % t == 0
    return t


def _encoder_layer(x, w):
    b, s, d = x.shape
    n = b * s
    tk = _tile(s, 512)
    tables = _rope_tables(s)
    q, k, vt, qm, km, vmt = _pre_attention(x, tables, w, tk)
    tq = _tile(s, 2048)
    oa = _attention(q, k, vt, _GQA_MAPS, tq)
    ob = _attention(qm, km, vmt, _MLA_MAPS, tq)
    x1, h2, afft, aff_tok = _post_attention(x, oa, ob, w, _tile(s, 512))

    cap = CAPACITY_FACTOR * n // N_EXPERTS
    pos = _select(afft, cap)
    xe_even, xe_odd = _dispatch(h2.reshape(n, d), pos, cap, _tile(cap // 2, 1024))
    ye = _expert_ffn(xe_even, xe_odd, w)
    ctile = _tile(cap // 2, 256)
    y = _combine(x1.reshape(n, d), pos.T, aff_tok.reshape(n, N_EXPERTS), ye,
                 _tile_starts(pos, ctile), ctile)
    return y.reshape(b, s, d)


def kernel(x_prompt, x_sample, norm_attn, w_in, gqa_q_norm, gqa_k_norm, mla_q_a_norm, mla_w_uq,
           mla_q_nope_norm, mla_q_pe_norm, mla_kv_a_norm, mla_w_ukv, mla_k_nope_norm,
           mla_k_pe_norm, out_norm_a, out_norm_b, w_o, norm_ffn, w_router, w_gate, w_up, w_down):
    p = dict(norm_attn=norm_attn, w_in=w_in, gqa_q_norm=gqa_q_norm, gqa_k_norm=gqa_k_norm,
             mla_q_a_norm=mla_q_a_norm, mla_w_uq=mla_w_uq, mla_q_nope_norm=mla_q_nope_norm,
             mla_q_pe_norm=mla_q_pe_norm, mla_kv_a_norm=mla_kv_a_norm, mla_w_ukv=mla_w_ukv,
             mla_k_nope_norm=mla_k_nope_norm, mla_k_pe_norm=mla_k_pe_norm,
             out_norm_a=out_norm_a, out_norm_b=out_norm_b, w_o=w_o, norm_ffn=norm_ffn,
             w_router=w_router, w_gate=w_gate, w_up=w_up, w_down=w_down)
    w = _prep_weights(p)
    return (_encoder_layer(x_prompt, w), _encoder_layer(x_sample, w))
```

```python
import functools
import math

import jax
import jax.numpy as jnp
from jax import lax
from jax.experimental import pallas as pl
from jax.experimental.pallas import tpu as pltpu

D_MODEL = 1024
GRID_W = 64
ROPE_THETA = 10000.0
RMS_EPS = 1e-6
GQA_HEADS = 8
GQA_KV_HEADS = 2
GQA_HEAD_DIM = 64
MLA_HEADS = 8
MLA_Q_LORA = 256
MLA_KV_LORA = 128
MLA_NOPE = 64
MLA_ROPE = 32
MLA_V = 64
N_EXPERTS = 16
CAPACITY_FACTOR = 2
D_FF_EXPERT = 1024

LANES = 128
HALF = LANES // 2
D_IN_PAD = 1280
VMEM_LIMIT = 56 * 1024 * 1024
LOG2E = math.log2(math.e)
ONES_ROWS = 16
MAX_LAG_EXPONENT = 60.0

F32 = jnp.float32
BF16 = jnp.bfloat16


def _cparams(sem):
    return pltpu.CompilerParams(dimension_semantics=sem, vmem_limit_bytes=VMEM_LIMIT)


def _rope_tables(seq_len):
    rows = seq_len // GRID_W
    row = jnp.repeat(jnp.arange(rows, dtype=F32), GRID_W)
    col = jnp.tile(jnp.arange(GRID_W, dtype=F32), rows)

    def angles(rot_dim):
        half = rot_dim // 2
        inv = ROPE_THETA ** (-jnp.arange(0, half, 2, dtype=F32) / half)
        ar = row[:, None] * inv[None, :]
        ac = col[:, None] * inv[None, :]
        return jnp.concatenate([ar, ar, ac, ac], axis=-1)

    lane = jnp.arange(LANES)
    a64 = angles(GQA_HEAD_DIM)
    cos_a = jnp.tile(jnp.cos(a64), (1, 2))
    sin_a = jnp.tile(jnp.sin(a64), (1, 2))
    low_a = (lane % 32) < 16
    up_a = jnp.where(low_a, -sin_a, 0.0)
    dn_a = jnp.where(low_a, 0.0, sin_a)

    a32 = angles(MLA_ROPE)
    pad = ((0, 0), (MLA_NOPE, LANES - MLA_NOPE - MLA_ROPE))
    cos_b = jnp.pad(jnp.cos(a32) - 1.0, pad) + 1.0
    sin_b = jnp.pad(jnp.sin(a32), pad)
    low_b = ((lane - MLA_NOPE) % 16) < 8
    up_b = jnp.where(low_b, -sin_b, 0.0)
    dn_b = jnp.where(low_b, 0.0, sin_b)
    return jnp.stack([cos_a, up_a, dn_a, cos_b, up_b, dn_b])


def _segment_mean_matrix(groups):
    m = jnp.zeros((LANES, LANES), F32)
    for start, width in groups:
        m = m.at[start:start + width, start:start + width].set(1.0 / width)
    return m.astype(BF16)


def _prep_weights(p):
    w_in = p['w_in']
    kpe_cols = jnp.pad(w_in[:, 1152:1184], ((0, 0), (MLA_NOPE, LANES - MLA_NOPE - MLA_ROPE)))
    w_in_p = jnp.concatenate([w_in[:, :1152], kpe_cols], axis=1).astype(BF16)
    w_uq = p['mla_w_uq'].reshape(MLA_Q_LORA, MLA_HEADS, MLA_NOPE + MLA_ROPE)
    w_uq_p = jnp.pad(w_uq, ((0, 0), (0, 0), (0, LANES - MLA_NOPE - MLA_ROPE)))
    w_uq_p = w_uq_p.reshape(MLA_Q_LORA, MLA_HEADS * LANES).astype(BF16)
    w_ukv = p['mla_w_ukv'].reshape(MLA_KV_LORA, MLA_HEADS, MLA_NOPE + MLA_V)
    w_uk_p = jnp.pad(w_ukv[:, :, :MLA_NOPE], ((0, 0), (0, 0), (0, LANES - MLA_NOPE)))
    w_uk_p = w_uk_p.reshape(MLA_KV_LORA, MLA_HEADS * LANES).astype(BF16)
    w_uv = w_ukv[:, :, MLA_NOPE:].reshape(MLA_KV_LORA, MLA_HEADS * MLA_V).astype(BF16)

    def row(v):
        return v.reshape(1, -1).astype(F32)

    zeros32 = jnp.zeros((LANES - MLA_NOPE - MLA_ROPE,), F32)
    zeros64 = jnp.zeros((HALF,), F32)
    return dict(
        w_in=w_in_p, w_uq=w_uq_p, w_uk=w_uk_p, w_uv=w_uv,
        g_attn=row(p['norm_attn']),
        g_q=row(jnp.tile(p['gqa_q_norm'], 2)),
        g_k=row(jnp.tile(p['gqa_k_norm'], 2)),
        g_cq=row(p['mla_q_a_norm']),
        g_qm=row(jnp.concatenate([p['mla_q_nope_norm'], p['mla_q_pe_norm'], zeros32])),
        g_ckv=row(p['mla_kv_a_norm']),
        g_kn=row(jnp.concatenate([p['mla_k_nope_norm'], zeros64])),
        g_kpe=row(jnp.concatenate([zeros64, p['mla_k_pe_norm'], zeros32])),
        s_gqa=_segment_mean_matrix([(0, HALF), (HALF, HALF)]),
        s_mla=_segment_mean_matrix([(0, MLA_NOPE), (MLA_NOPE, MLA_ROPE)]),
        g_oa=row(p['out_norm_a']), g_ob=row(p['out_norm_b']),
        w_o=p['w_o'].astype(BF16),
        g_ffn=row(p['norm_ffn']),
        w_r=jnp.pad(p['w_router'], ((0, 0), (0, LANES - N_EXPERTS))).astype(BF16),
        w_gate=p['w_gate'].astype(BF16), w_up=p['w_up'].astype(BF16),
        w_down=p['w_down'].astype(BF16),
    )


def _rms(x, gain):
    ms = jnp.mean(x * x, axis=-1, keepdims=True)
    return x * lax.rsqrt(ms + RMS_EPS) * gain


def _segment_rms(x, seg_mat, gain):
    ms = jnp.dot((x * x).astype(BF16), seg_mat, preferred_element_type=F32)
    return x * lax.rsqrt(ms + RMS_EPS) * gain


def _rope(x, cos, up, dn, quarter):
    return (x * cos + pltpu.roll(x, LANES - quarter, 1) * up + pltpu.roll(x, quarter, 1) * dn)


def _pre_attn_kernel(x_ref, tab_ref, g_attn, w_in, g_q, g_k, g_cq, w_uq, g_qm, g_ckv, w_uk, w_uv,
                     g_kn, g_kpe, s_gqa, s_mla,
                     q_ref, k_ref, v_ref, qm_ref, km_ref, vm_ref):
    cos_a, up_a, dn_a = tab_ref[0], tab_ref[1], tab_ref[2]
    cos_b, up_b, dn_b = tab_ref[3], tab_ref[4], tab_ref[5]
    hn = _rms(x_ref[...], g_attn[...]).astype(BF16)
    proj = jnp.dot(hn, w_in[...], preferred_element_type=F32)

    sg = s_gqa[...]
    sm = s_mla[...]
    q_scale = LOG2E / math.sqrt(GQA_HEAD_DIM)
    for c in range(GQA_HEADS // 2):
        xc = _segment_rms(proj[:, c * LANES:(c + 1) * LANES], sg, g_q[...])
        q_ref[c] = (_rope(xc, cos_a, up_a, dn_a, 16) * q_scale).astype(BF16)

    low = lax.broadcasted_iota(jnp.int32, (1, LANES), 1) < HALF
    k01 = _rope(_segment_rms(proj[:, 512:640], sg, g_k[...]), cos_a, up_a, dn_a, 16)
    k10 = pltpu.roll(k01, HALF, 1)
    k_ref[0] = jnp.where(low, k01, 0.0).astype(BF16)
    k_ref[1] = jnp.where(low, 0.0, k10).astype(BF16)
    k_ref[2] = jnp.where(low, k10, 0.0).astype(BF16)
    k_ref[3] = jnp.where(low, 0.0, k01).astype(BF16)
    v01 = proj[:, 640:768]
    v_ref[0] = jnp.transpose(v01).astype(BF16)
    v_ref[1] = jnp.transpose(pltpu.roll(v01, HALF, 1)).astype(BF16)

    cq = _rms(proj[:, 768:1024], g_cq[...]).astype(BF16)
    qb = jnp.dot(cq, w_uq[...], preferred_element_type=F32)
    m_scale = LOG2E / math.sqrt(MLA_NOPE + MLA_ROPE)
    ckv = _rms(proj[:, 1024:1152], g_ckv[...]).astype(BF16)
    kn = jnp.dot(ckv, w_uk[...], preferred_element_type=F32)
    vv = jnp.dot(ckv, w_uv[...], preferred_element_type=F32)
    kpe = _rope(_segment_rms(proj[:, 1152:1280], sm, g_kpe[...]), cos_b, up_b, dn_b, 8)
    for h in range(MLA_HEADS):
        qh = _segment_rms(qb[:, h * LANES:(h + 1) * LANES], sm, g_qm[...])
        qm_ref[h] = (_rope(qh, cos_b, up_b, dn_b, 8) * m_scale).astype(BF16)
        kh = _segment_rms(kn[:, h * LANES:(h + 1) * LANES], sm, g_kn[...])
        km_ref[h] = (kh + kpe).astype(BF16)
    for c in range(MLA_HEADS // 2):
        vm_ref[c] = jnp.transpose(vv[:, c * LANES:(c + 1) * LANES]).astype(BF16)


def _full(shape):
    return pl.BlockSpec(shape, lambda *_: (0,) * len(shape))


def _pre_attention(x, tables, w, tm):
    b, s, d = x.shape
    nt = s // tm
    grid = (b, nt)

    def head_major(n):
        return pl.BlockSpec((None, n, tm, LANES), lambda bi, i: (bi, 0, i, 0))

    weights = [w['g_attn'], w['w_in'], w['g_q'], w['g_k'], w['g_cq'], w['w_uq'], w['g_qm'],
               w['g_ckv'], w['w_uk'], w['w_uv'], w['g_kn'], w['g_kpe'], w['s_gqa'], w['s_mla']]
    in_specs = [pl.BlockSpec((None, tm, d), lambda bi, i: (bi, i, 0)),
                pl.BlockSpec((6, tm, LANES), lambda bi, i: (0, i, 0))]
    in_specs += [_full(a.shape) for a in weights]
    def head_major_t(n):
        return pl.BlockSpec((None, n, None, LANES, tm), lambda bi, i: (bi, 0, i, 0, 0))

    counts = (4, 4, 2, 8, 8, 4)
    transposed = (False, False, True, False, False, True)
    out_shape = [jax.ShapeDtypeStruct((b, n, nt, LANES, tm) if t else (b, n, s, LANES), BF16)
                 for n, t in zip(counts, transposed)]
    out_specs = [head_major_t(n) if t else head_major(n) for n, t in zip(counts, transposed)]
    return pl.pallas_call(
        _pre_attn_kernel, grid=grid, in_specs=in_specs, out_specs=out_specs, out_shape=out_shape,
        compiler_params=_cparams(("parallel", "parallel")),
    )(x, tables, *weights)


def _attn_kernel(qe_ref, qo_ref, ke_ref, ko_ref, ve_ref, vo_ref, o_ref, acc_sc, *, tk):
    n_chunks = ve_ref.shape[0]
    tq = acc_sc.shape[2]
    refs = ((qe_ref, ke_ref, ve_ref), (qo_ref, ko_ref, vo_ref))
    qts = [jnp.transpose(q_ref[...].astype(F32)).astype(BF16) for q_ref, _, _ in refs]
    ones_rows = jnp.ones((ONES_ROWS, tk), BF16)

    def scores(j):
        start = pl.multiple_of(j * tk, tk)
        return [jnp.dot(k_ref[pl.ds(start, tk), :], qts[h], preferred_element_type=F32)
                for h, (_, k_ref, _) in enumerate(refs)]

    def weighted_values(h, j, p):
        v_ref = refs[h][2]
        lhs = jnp.concatenate([v_ref[j][h * HALF:(h + 1) * HALF], ones_rows], axis=0)
        return jnp.dot(lhs, p.astype(BF16), preferred_element_type=F32)

    def lagged_body(j, carry):
        sts = scores(j)
        out = []
        for h in range(2):
            ref_max, excess = carry[h]
            chunk_max = jnp.max(sts[h], axis=0, keepdims=True)
            p = jnp.exp2(sts[h] - ref_max)
            new_max = jnp.maximum(ref_max, chunk_max)
            acc_sc[h] = jnp.exp2(ref_max - new_max) * (acc_sc[h] + weighted_values(h, j, p))
            out.append((new_max, jnp.maximum(excess, chunk_max - ref_max)))
        return tuple(out)

    def exact_body(j, carry):
        sts = scores(j)
        out = []
        for h in range(2):
            new_max = jnp.maximum(carry[h], jnp.max(sts[h], axis=0, keepdims=True))
            p = jnp.exp2(sts[h] - new_max)
            acc_sc[h] = jnp.exp2(carry[h] - new_max) * acc_sc[h] + weighted_values(h, j, p)
            out.append(new_max)
        return tuple(out)

    def run(body, init):
        acc_sc[...] = jnp.zeros(acc_sc.shape, F32)
        carry = lax.fori_loop(0, n_chunks, body, init)
        ot = jnp.concatenate([acc_sc[h, :HALF] / acc_sc[h, HALF:HALF + 1] for h in range(2)],
                             axis=0)
        o_ref[...] = jnp.transpose(ot)
        return carry

    first_ref = [jnp.max(jnp.dot(k_ref[0:16, :], qts[h], preferred_element_type=F32),
                         axis=0, keepdims=True) for h, (_, k_ref, _) in enumerate(refs)]
    zero = jnp.zeros((1, tq), F32)
    (_, x_even), (_, x_odd) = run(lagged_body, tuple((r, zero) for r in first_ref))
    safe = jnp.max(jnp.maximum(x_even, x_odd)) < MAX_LAG_EXPONENT

    @pl.when(jnp.logical_not(safe))
    def _():
        neg_inf = jnp.full((1, tq), -jnp.inf, F32)
        run(exact_body, (neg_inf, neg_inf))


def _attention(q_arr, k_arr, vt_arr, maps, tq):
    b, _, s, _ = q_arr.shape
    n_pairs = 4
    n_chunks, _, tk = vt_arr.shape[2:]
    grid = (b, n_pairs, s // tq)
    qe, qo, ke, ko, ve, vo = maps

    def qspec(f):
        return pl.BlockSpec((None, None, tq, LANES), lambda bi, p, i: (bi, f(p), i, 0))

    def kspec(f):
        return pl.BlockSpec((None, None, s, LANES), lambda bi, p, i: (bi, f(p), 0, 0))

    def vspec(f):
        return pl.BlockSpec((None, None, n_chunks, LANES, tk),
                            lambda bi, p, i: (bi, f(p), 0, 0, 0))

    return pl.pallas_call(
        functools.partial(_attn_kernel, tk=tk), grid=grid,
        in_specs=[qspec(qe), qspec(qo), kspec(ke), kspec(ko), vspec(ve), vspec(vo)],
        out_specs=pl.BlockSpec((None, tq, LANES), lambda bi, p, i: (bi, i, p)),
        out_shape=jax.ShapeDtypeStruct((b, s, n_pairs * LANES), F32),
        scratch_shapes=[pltpu.VMEM((2, HALF + ONES_ROWS, tq), F32)],
        compiler_params=_cparams(("parallel", "parallel", "parallel")),
    )(q_arr, q_arr, k_arr, k_arr, vt_arr, vt_arr)


_GQA_MAPS = (lambda p: p, lambda p: p,
             lambda p: 2 * (p // 2), lambda p: 2 * (p // 2) + 1,
             lambda p: p // 2, lambda p: 1 - p // 2)
_MLA_MAPS = (lambda p: 2 * p, lambda p: 2 * p + 1,
             lambda p: 2 * p, lambda p: 2 * p + 1,
             lambda p: p, lambda p: p)


def _post_attn_kernel(x_ref, oa_ref, ob_ref, g_oa, g_ob, w_o, g_ffn, w_r,
                      x1_ref, h2_ref, afft_ref, aff_ref):
    na = _rms(oa_ref[...], g_oa[...]).astype(BF16)
    nb = _rms(ob_ref[...], g_ob[...]).astype(BF16)
    half = na.shape[1]
    mix = (jnp.dot(na, w_o[:half, :], preferred_element_type=F32)
           + jnp.dot(nb, w_o[half:, :], preferred_element_type=F32))
    x1 = x_ref[...] + mix
    x1_ref[...] = x1
    h2 = _rms(x1, g_ffn[...]).astype(BF16)
    h2_ref[...] = h2
    logits = jnp.dot(h2, w_r[...], preferred_element_type=F32)
    valid = lax.broadcasted_iota(jnp.int32, (1, LANES), 1) < N_EXPERTS
    logits = jnp.where(valid, logits, -jnp.inf)
    e = jnp.exp(logits - jnp.max(logits, axis=-1, keepdims=True))
    aff = e / jnp.sum(e, axis=-1, keepdims=True)
    afft_ref[...] = jnp.transpose(aff)[:N_EXPERTS, :]
    aff_ref[...] = aff[:, :N_EXPERTS]


def _post_attention(x, oa, ob, w, tm):
    b, s, d = x.shape
    nt = s // tm
    tok = lambda width: pl.BlockSpec((None, tm, width), lambda bi, i: (bi, i, 0))
    weights = [w['g_oa'], w['g_ob'], w['w_o'], w['g_ffn'], w['w_r']]
    return pl.pallas_call(
        _post_attn_kernel, grid=(b, nt),
        in_specs=[tok(d), tok(oa.shape[2]), tok(ob.shape[2])] + [_full(a.shape) for a in weights],
        out_specs=[tok(d), tok(d),
                   pl.BlockSpec((N_EXPERTS, tm), lambda bi, i: (0, bi * nt + i)),
                   tok(N_EXPERTS)],
        out_shape=[jax.ShapeDtypeStruct((b, s, d), F32), jax.ShapeDtypeStruct((b, s, d), BF16),
                   jax.ShapeDtypeStruct((N_EXPERTS, b * s), F32),
                   jax.ShapeDtypeStruct((b, s, N_EXPERTS), F32)],
        compiler_params=_cparams(("parallel", "parallel")),
    )(x, oa, ob, *weights)


SLOT_CHUNK = LANES


def _threshold_kernel(aff_ref, thr_ref, need_ref, *, cap):
    def count(mask):
        return jnp.sum(jnp.where(mask, 1.0, 0.0), axis=1, keepdims=True)

    def body(i, prefix):
        cand = prefix | jnp.left_shift(jnp.int32(1), 30 - i)
        cnt = count(pltpu.bitcast(aff_ref[...], jnp.int32) >= cand)
        return jnp.where(cnt >= cap, cand, prefix)

    thr = lax.fori_loop(0, 31, body, jnp.zeros((N_EXPERTS, 1), jnp.int32))
    greater = count(pltpu.bitcast(aff_ref[...], jnp.int32) > thr)
    thr_ref[...] = thr
    need_ref[...] = (cap - greater).astype(jnp.int32)


def _position_kernel(thr_ref, need_ref, aff_ref, upper_ref, lower_ref, pos_ref):
    e = pl.program_id(0)
    bits = pltpu.bitcast(aff_ref[...], jnp.int32)
    thr = thr_ref[e]
    need = need_ref[e].astype(F32)
    upper = upper_ref[...]
    lower = lower_ref[...]

    def inclusive_cumsum(mask):
        m = jnp.where(mask, 1.0, 0.0).astype(BF16)
        within = jnp.dot(m, upper, preferred_element_type=F32)
        above = jnp.dot(lower, m, preferred_element_type=F32)
        return within + jnp.sum(above, axis=1, keepdims=True)

    greater = bits > thr
    equal = bits == thr
    equal_rank = inclusive_cumsum(equal) - jnp.where(equal, 1.0, 0.0)
    selected = greater | (equal & (equal_rank < need))
    pos = inclusive_cumsum(selected) - 1.0
    pos_ref[...] = jnp.where(selected, pos, -1.0).astype(jnp.int32)


def _select(afft, cap):
    e, n = afft.shape
    nb = n // LANES
    thr, need = pl.pallas_call(
        functools.partial(_threshold_kernel, cap=cap),
        out_shape=[jax.ShapeDtypeStruct((e, 1), jnp.int32)] * 2,
        compiler_params=pltpu.CompilerParams(vmem_limit_bytes=VMEM_LIMIT),
    )(afft)
    upper = jnp.triu(jnp.ones((LANES, LANES), F32)).astype(BF16)
    lower = jnp.tril(jnp.ones((nb, nb), F32), -1).astype(BF16)
    pos = pl.pallas_call(
        _position_kernel,
        grid_spec=pltpu.PrefetchScalarGridSpec(
            num_scalar_prefetch=2, grid=(e,),
            in_specs=[pl.BlockSpec((nb, LANES), lambda ei, *_: (ei, 0)),
                      pl.BlockSpec((LANES, LANES), lambda ei, *_: (0, 0)),
                      pl.BlockSpec((nb, nb), lambda ei, *_: (0, 0))],
            out_specs=pl.BlockSpec((nb, LANES), lambda ei, *_: (ei, 0))),
        out_shape=jax.ShapeDtypeStruct((e * nb, LANES), jnp.int32),
        compiler_params=_cparams(("parallel",)),
    )(thr.reshape(e), need.reshape(e), afft.reshape(e * nb, LANES), upper, lower)
    return pos.reshape(e, n)


def _tile_starts(pos, tile):
    e, n = pos.shape
    cnt = jnp.sum((pos >= 0).reshape(e, n // tile, tile), axis=-1, dtype=jnp.int32)
    return jnp.concatenate([jnp.zeros((e, 1), jnp.int32), jnp.cumsum(cnt, axis=1)], axis=1)


def _dispatch_kernel(even_ref, odd_ref, st_ref, pos_ref, h_ref, xe_even, xe_odd, *, nt, bs):
    e = pl.program_id(0)
    t = pl.program_id(1)
    i = e * nt + t
    prev = jnp.maximum(i - 1, 0)

    @pl.when((t == 0) | (even_ref[i] != even_ref[prev]))
    def _():
        xe_even[...] = jnp.zeros(xe_even.shape, xe_even.dtype)

    @pl.when((t == 0) | (odd_ref[i] != odd_ref[prev]))
    def _():
        xe_odd[...] = jnp.zeros(xe_odd.shape, xe_odd.dtype)

    s0 = st_ref[e * (nt + 1) + t]
    s1 = st_ref[e * (nt + 1) + t + 1]
    chunks_per_block = bs // SLOT_CHUNK

    def body(g, carry):
        slot = g * SLOT_CHUNK + lax.broadcasted_iota(jnp.int32, (SLOT_CHUNK, 1), 0)
        onehot = jnp.where(pos_ref[...] == slot, 1.0, 0.0).astype(BF16)
        val = jnp.dot(onehot, h_ref[...], preferred_element_type=F32).astype(xe_even.dtype)
        block = g // chunks_per_block
        off = pl.multiple_of((g % chunks_per_block) * SLOT_CHUNK, SLOT_CHUNK)

        @pl.when(block % 2 == 0)
        def _():
            xe_even[pl.ds(off, SLOT_CHUNK), :] += val

        @pl.when(block % 2 == 1)
        def _():
            xe_odd[pl.ds(off, SLOT_CHUNK), :] += val

        return carry

    lax.fori_loop(s0 // SLOT_CHUNK, (s1 + SLOT_CHUNK - 1) // SLOT_CHUNK, body, 0)


def _dispatch(h2, pos, cap, bs):
    n, d = h2.shape
    e = pos.shape[0]
    tile = bs
    nt = n // tile
    nb = cap // bs
    assert cap % (2 * bs) == 0 and n % tile == 0
    starts = _tile_starts(pos, tile)
    b0 = starts[:, :-1] // bs
    even_idx = ((b0 + 1) // 2).reshape(-1)
    odd_idx = (b0 // 2).reshape(-1)
    return pl.pallas_call(
        functools.partial(_dispatch_kernel, nt=nt, bs=bs),
        grid_spec=pltpu.PrefetchScalarGridSpec(
            num_scalar_prefetch=3, grid=(e, nt),
            in_specs=[pl.BlockSpec((None, 1, tile), lambda ei, t, *_: (ei, 0, t)),
                      pl.BlockSpec((tile, d), lambda ei, t, *_: (t, 0))],
            out_specs=[pl.BlockSpec((None, None, bs, d),
                                    lambda ei, t, ev, od, st: (ei, ev[ei * nt + t], 0, 0)),
                       pl.BlockSpec((None, None, bs, d),
                                    lambda ei, t, ev, od, st: (ei, od[ei * nt + t], 0, 0))]),
        out_shape=[jax.ShapeDtypeStruct((e, nb // 2 + 1, bs, d), BF16),
                   jax.ShapeDtypeStruct((e, nb // 2, bs, d), BF16)],
        compiler_params=_cparams(("arbitrary", "arbitrary")),
    )(even_idx, odd_idx, starts.reshape(-1), pos.reshape(e, 1, n), h2)


def _combine_kernel(st_ref, x1_ref, pos_ref, aff_ref, *refs, nt, by):
    y_ref = refs[-1]
    t = pl.program_id(0)

    def contribution(e, j):
        lo = (st_ref[e * (nt + 1) + t] // by + j) * by
        slot = lo + lax.broadcasted_iota(jnp.int32, (1, by), 1)
        onehot = jnp.where(pos_ref[:, e:e + 1] == slot, 1.0, 0.0).astype(BF16)
        part = jnp.dot(onehot, refs[2 * e + j][...], preferred_element_type=F32)
        return aff_ref[:, e:e + 1] * part

    acc = x1_ref[...]
    for e in range(N_EXPERTS):
        acc = acc + contribution(e, 0)
    y_ref[...] = acc
    for e in range(N_EXPERTS):
        spills_over = st_ref[e * (nt + 1) + t + 1] > (st_ref[e * (nt + 1) + t] // by + 1) * by

        @pl.when(spills_over)
        def _(e=e):
            y_ref[...] += contribution(e, 1)


def _combine(x1, pos_tok, aff_tok, ye, starts, tile):
    n, d = x1.shape
    e, cap, _ = ye.shape
    by = tile
    nt = n // tile
    nby = cap // by

    def ye_spec(ei, j):
        def index(t, st):
            return (ei, jnp.minimum(st[ei * (nt + 1) + t] // by + j, nby - 1), 0)
        return pl.BlockSpec((None, by, d), index)

    tok = lambda width: pl.BlockSpec((tile, width), lambda t, st: (t, 0))
    ye_specs = [ye_spec(ei, j) for ei in range(e) for j in range(2)]
    return pl.pallas_call(
        functools.partial(_combine_kernel, nt=nt, by=by),
        grid_spec=pltpu.PrefetchScalarGridSpec(
            num_scalar_prefetch=1, grid=(nt,),
            in_specs=[tok(d), tok(e), tok(e)] + ye_specs,
            out_specs=tok(d)),
        out_shape=jax.ShapeDtypeStruct((n, d), F32),
        compiler_params=_cparams(("arbitrary",)),
    )(starts.reshape(-1), x1, pos_tok, aff_tok, *([ye] * (2 * e)))


def _expert_kernel(xe_even, xe_odd, wg_ref, wu_ref, wd_ref, ye_ref):
    xe = jnp.where(pl.program_id(1) % 2 == 0, xe_even[...], xe_odd[...])
    a = jnp.dot(xe, wg_ref[...], preferred_element_type=F32)
    u = jnp.dot(xe, wu_ref[...], preferred_element_type=F32)
    hid = (a * jax.nn.sigmoid(a) * u).astype(BF16)
    ye_ref[...] = jnp.dot(hid, wd_ref[...], preferred_element_type=F32).astype(ye_ref.dtype)


def _expert_ffn(xe_even, xe_odd, w):
    e, nbh, bs, d = xe_odd.shape
    f = w['w_gate'].shape[2]
    xspec = pl.BlockSpec((None, None, bs, d), lambda ei, ci: (ei, ci // 2, 0, 0))
    wspec = lambda r, cc: pl.BlockSpec((None, r, cc), lambda ei, ci: (ei, 0, 0))
    return pl.pallas_call(
        _expert_kernel, grid=(e, 2 * nbh),
        in_specs=[xspec, xspec, wspec(d, f), wspec(d, f), wspec(f, d)],
        out_specs=pl.BlockSpec((None, bs, d), lambda ei, ci: (ei, ci, 0)),
        out_shape=jax.ShapeDtypeStruct((e, 2 * nbh * bs, d), BF16),
        compiler_params=_cparams(("parallel", "parallel")),
    )(xe_even, xe_odd, w['w_gate'], w['w_up'], w['w_down'])


def _tile(n, pref):
    t = min(n, pref)
    assert n % t == 0
    return t


def _encoder_layer(x, w):
    b, s, d = x.shape
    n = b * s
    tk = _tile(s, 1024)
    tables = _rope_tables(s)
    q, k, vt, qm, km, vmt = _pre_attention(x, tables, w, tk)
    tq = _tile(s, 2048)
    oa = _attention(q, k, vt, _GQA_MAPS, tq)
    ob = _attention(qm, km, vmt, _MLA_MAPS, tq)
    x1, h2, afft, aff_tok = _post_attention(x, oa, ob, w, _tile(s, 512))

    cap = CAPACITY_FACTOR * n // N_EXPERTS
    pos = _select(afft, cap)
    xe_even, xe_odd = _dispatch(h2.reshape(n, d), pos, cap, _tile(cap // 2, 1024))
    ye = _expert_ffn(xe_even, xe_odd, w)
    ctile = _tile(cap // 2, 256)
    y = _combine(x1.reshape(n, d), pos.T, aff_tok.reshape(n, N_EXPERTS), ye,
                 _tile_starts(pos, ctile), ctile)
    return y.reshape(b, s, d)


def kernel(x_prompt, x_sample, norm_attn, w_in, gqa_q_norm, gqa_k_norm, mla_q_a_norm, mla_w_uq,
           mla_q_nope_norm, mla_q_pe_norm, mla_kv_a_norm, mla_w_ukv, mla_k_nope_norm,
           mla_k_pe_norm, out_norm_a, out_norm_b, w_o, norm_ffn, w_router, w_gate, w_up, w_down):
    p = dict(norm_attn=norm_attn, w_in=w_in, gqa_q_norm=gqa_q_norm, gqa_k_norm=gqa_k_norm,
             mla_q_a_norm=mla_q_a_norm, mla_w_uq=mla_w_uq, mla_q_nope_norm=mla_q_nope_norm,
             mla_q_pe_norm=mla_q_pe_norm, mla_kv_a_norm=mla_kv_a_norm, mla_w_ukv=mla_w_ukv,
             mla_k_nope_norm=mla_k_nope_norm, mla_k_pe_norm=mla_k_pe_norm,
             out_norm_a=out_norm_a, out_norm_b=out_norm_b, w_o=w_o, norm_ffn=norm_ffn,
             w_router=w_router, w_gate=w_gate, w_up=w_up, w_down=w_down)
    w = _prep_weights(p)
    return (_encoder_layer(x_prompt, w), _encoder_layer(x_sample, w))
```

```python
import functools
import math

import jax
import jax.numpy as jnp
from jax import lax
from jax.experimental import pallas as pl
from jax.experimental.pallas import tpu as pltpu

D_MODEL = 1024
GRID_W = 64
ROPE_THETA = 10000.0
RMS_EPS = 1e-6
GQA_HEADS = 8
GQA_KV_HEADS = 2
GQA_HEAD_DIM = 64
MLA_HEADS = 8
MLA_Q_LORA = 256
MLA_KV_LORA = 128
MLA_NOPE = 64
MLA_ROPE = 32
MLA_V = 64
N_EXPERTS = 16
CAPACITY_FACTOR = 2
D_FF_EXPERT = 1024

LANES = 128
HALF = LANES // 2
D_IN_PAD = 1280
VMEM_LIMIT = 56 * 1024 * 1024
LOG2E = math.log2(math.e)
ONES_ROWS = 16
MAX_LAG_EXPONENT = 60.0

F32 = jnp.float32
BF16 = jnp.bfloat16


def _cparams(sem):
    return pltpu.CompilerParams(dimension_semantics=sem, vmem_limit_bytes=VMEM_LIMIT)


def _rope_tables(seq_len):
    rows = seq_len // GRID_W
    row = jnp.repeat(jnp.arange(rows, dtype=F32), GRID_W)
    col = jnp.tile(jnp.arange(GRID_W, dtype=F32), rows)

    def angles(rot_dim):
        half = rot_dim // 2
        inv = ROPE_THETA ** (-jnp.arange(0, half, 2, dtype=F32) / half)
        ar = row[:, None] * inv[None, :]
        ac = col[:, None] * inv[None, :]
        return jnp.concatenate([ar, ar, ac, ac], axis=-1)

    lane = jnp.arange(LANES)
    a64 = angles(GQA_HEAD_DIM)
    cos_a = jnp.tile(jnp.cos(a64), (1, 2))
    sin_a = jnp.tile(jnp.sin(a64), (1, 2))
    low_a = (lane % 32) < 16
    up_a = jnp.where(low_a, -sin_a, 0.0)
    dn_a = jnp.where(low_a, 0.0, sin_a)

    a32 = angles(MLA_ROPE)
    pad = ((0, 0), (MLA_NOPE, LANES - MLA_NOPE - MLA_ROPE))
    cos_b = jnp.pad(jnp.cos(a32) - 1.0, pad) + 1.0
    sin_b = jnp.pad(jnp.sin(a32), pad)
    low_b = ((lane - MLA_NOPE) % 16) < 8
    up_b = jnp.where(low_b, -sin_b, 0.0)
    dn_b = jnp.where(low_b, 0.0, sin_b)
    return jnp.stack([cos_a, up_a, dn_a, cos_b, up_b, dn_b])


def _segment_mean_matrix(groups):
    m = jnp.zeros((2 * LANES, 2 * LANES), F32)
    for base in (0, LANES):
        for start, width in groups:
            lo = base + start
            m = m.at[lo:lo + width, lo:lo + width].set(1.0 / width)
    return m.astype(BF16)


def _prep_weights(p):
    w_in = p['w_in']
    kpe_cols = jnp.pad(w_in[:, 1152:1184], ((0, 0), (MLA_NOPE, LANES - MLA_NOPE - MLA_ROPE)))
    w_in_p = jnp.concatenate([w_in[:, :1152], kpe_cols], axis=1).astype(BF16)
    w_uq = p['mla_w_uq'].reshape(MLA_Q_LORA, MLA_HEADS, MLA_NOPE + MLA_ROPE)
    w_uq_p = jnp.pad(w_uq, ((0, 0), (0, 0), (0, LANES - MLA_NOPE - MLA_ROPE)))
    w_uq_p = w_uq_p.reshape(MLA_Q_LORA, MLA_HEADS * LANES).astype(BF16)
    w_ukv = p['mla_w_ukv'].reshape(MLA_KV_LORA, MLA_HEADS, MLA_NOPE + MLA_V)
    w_uk_p = jnp.pad(w_ukv[:, :, :MLA_NOPE], ((0, 0), (0, 0), (0, LANES - MLA_NOPE)))
    w_uk_p = w_uk_p.reshape(MLA_KV_LORA, MLA_HEADS * LANES).astype(BF16)
    w_uv = w_ukv[:, :, MLA_NOPE:].reshape(MLA_KV_LORA, MLA_HEADS * MLA_V).astype(BF16)

    def row(v):
        return v.reshape(1, -1).astype(F32)

    zeros32 = jnp.zeros((LANES - MLA_NOPE - MLA_ROPE,), F32)
    zeros64 = jnp.zeros((HALF,), F32)
    return dict(
        w_in=w_in_p, w_uq=w_uq_p, w_uk=w_uk_p, w_uv=w_uv,
        g_attn=row(p['norm_attn']),
        g_q=row(jnp.tile(p['gqa_q_norm'], 2)),
        g_k=row(jnp.tile(p['gqa_k_norm'], 2)),
        g_cq=row(p['mla_q_a_norm']),
        g_qm=row(jnp.concatenate([p['mla_q_nope_norm'], p['mla_q_pe_norm'], zeros32])),
        g_ckv=row(p['mla_kv_a_norm']),
        g_kn=row(jnp.concatenate([p['mla_k_nope_norm'], zeros64])),
        g_kpe=row(jnp.concatenate([zeros64, p['mla_k_pe_norm'], zeros32])),
        s_gqa=_segment_mean_matrix([(0, HALF), (HALF, HALF)]),
        s_mla=_segment_mean_matrix([(0, MLA_NOPE), (MLA_NOPE, MLA_ROPE)]),
        g_oa=row(p['out_norm_a']), g_ob=row(p['out_norm_b']),
        w_o=p['w_o'].astype(BF16),
        g_ffn=row(p['norm_ffn']),
        w_r=jnp.pad(p['w_router'], ((0, 0), (0, LANES - N_EXPERTS))).astype(BF16),
        w_gate=p['w_gate'].astype(BF16), w_up=p['w_up'].astype(BF16),
        w_down=p['w_down'].astype(BF16),
    )


def _rms(x, gain):
    ms = jnp.mean(x * x, axis=-1, keepdims=True)
    return x * lax.rsqrt(ms + RMS_EPS) * gain


def _segment_inv_rms(x, seg_mat):
    width = seg_mat.shape[0]
    sq = (x * x).astype(BF16)
    ms = [jnp.dot(sq[:, c:c + width], seg_mat, preferred_element_type=F32)
          for c in range(0, x.shape[1], width)]
    return lax.rsqrt((ms[0] if len(ms) == 1 else jnp.concatenate(ms, axis=1)) + RMS_EPS)


def _rope(x, cos, up, dn, quarter):
    return (x * cos + pltpu.roll(x, LANES - quarter, 1) * up + pltpu.roll(x, quarter, 1) * dn)


def _pre_attn_kernel(x_ref, tab_ref, g_attn, w_in, g_q, g_k, g_cq, w_uq, g_qm, g_ckv, w_uk, w_uv,
                     g_kn, g_kpe, s_gqa, s_mla,
                     q_ref, k_ref, v_ref, qm_ref, km_ref, vm_ref):
    cos_a, up_a, dn_a = tab_ref[0], tab_ref[1], tab_ref[2]
    cos_b, up_b, dn_b = tab_ref[3], tab_ref[4], tab_ref[5]
    hn = _rms(x_ref[...], g_attn[...]).astype(BF16)
    proj = jnp.dot(hn, w_in[...], preferred_element_type=F32)

    lanes = lambda a, c: a[:, c * LANES:(c + 1) * LANES]
    sg = s_gqa[...]
    sm = s_mla[...]
    q_scale = LOG2E / math.sqrt(GQA_HEAD_DIM)
    gq = proj[:, :512]
    gq_inv = _segment_inv_rms(gq, sg)
    for c in range(GQA_HEADS // 2):
        xc = lanes(gq, c) * lanes(gq_inv, c) * g_q[...]
        q_ref[c] = (_rope(xc, cos_a, up_a, dn_a, 16) * q_scale).astype(BF16)

    low = lax.broadcasted_iota(jnp.int32, (1, LANES), 1) < HALF
    gk = proj[:, 512:640]
    k01 = _rope(gk * _segment_inv_rms(gk, s_gqa[:LANES, :LANES]) * g_k[...], cos_a, up_a, dn_a, 16)
    k10 = pltpu.roll(k01, HALF, 1)
    k_ref[0] = jnp.where(low, k01, 0.0).astype(BF16)
    k_ref[1] = jnp.where(low, 0.0, k10).astype(BF16)
    k_ref[2] = jnp.where(low, k10, 0.0).astype(BF16)
    k_ref[3] = jnp.where(low, 0.0, k01).astype(BF16)
    v01 = proj[:, 640:768]
    v_ref[0] = jnp.transpose(v01).astype(BF16)
    v_ref[1] = jnp.transpose(pltpu.roll(v01, HALF, 1)).astype(BF16)

    cq = _rms(proj[:, 768:1024], g_cq[...]).astype(BF16)
    qb = jnp.dot(cq, w_uq[...], preferred_element_type=F32)
    m_scale = LOG2E / math.sqrt(MLA_NOPE + MLA_ROPE)
    ckv = _rms(proj[:, 1024:1152], g_ckv[...]).astype(BF16)
    kn = jnp.dot(ckv, w_uk[...], preferred_element_type=F32)
    vv = jnp.dot(ckv, w_uv[...], preferred_element_type=F32)
    gkpe = proj[:, 1152:1280]
    kpe = _rope(gkpe * _segment_inv_rms(gkpe, s_mla[:LANES, :LANES]) * g_kpe[...],
                cos_b, up_b, dn_b, 8)
    qb_inv = _segment_inv_rms(qb, sm)
    kn_inv = _segment_inv_rms(kn, sm)
    for h in range(MLA_HEADS):
        qh = lanes(qb, h) * lanes(qb_inv, h) * g_qm[...]
        qm_ref[h] = (_rope(qh, cos_b, up_b, dn_b, 8) * m_scale).astype(BF16)
        km_ref[h] = (lanes(kn, h) * lanes(kn_inv, h) * g_kn[...] + kpe).astype(BF16)
    for c in range(MLA_HEADS // 2):
        vm_ref[c] = jnp.transpose(vv[:, c * LANES:(c + 1) * LANES]).astype(BF16)


def _full(shape):
    return pl.BlockSpec(shape, lambda *_: (0,) * len(shape))


def _pre_attention(x, tables, w, tm):
    b, s, d = x.shape
    nt = s // tm
    grid = (b, nt)

    def head_major(n):
        return pl.BlockSpec((None, n, tm, LANES), lambda bi, i: (bi, 0, i, 0))

    weights = [w['g_attn'], w['w_in'], w['g_q'], w['g_k'], w['g_cq'], w['w_uq'], w['g_qm'],
               w['g_ckv'], w['w_uk'], w['w_uv'], w['g_kn'], w['g_kpe'], w['s_gqa'], w['s_mla']]
    in_specs = [pl.BlockSpec((None, tm, d), lambda bi, i: (bi, i, 0)),
                pl.BlockSpec((6, tm, LANES), lambda bi, i: (0, i, 0))]
    in_specs += [_full(a.shape) for a in weights]
    def head_major_t(n):
        return pl.BlockSpec((None, n, None, LANES, tm), lambda bi, i: (bi, 0, i, 0, 0))

    counts = (4, 4, 2, 8, 8, 4)
    transposed = (False, False, True, False, False, True)
    out_shape = [jax.ShapeDtypeStruct((b, n, nt, LANES, tm) if t else (b, n, s, LANES), BF16)
                 for n, t in zip(counts, transposed)]
    out_specs = [head_major_t(n) if t else head_major(n) for n, t in zip(counts, transposed)]
    return pl.pallas_call(
        _pre_attn_kernel, grid=grid, in_specs=in_specs, out_specs=out_specs, out_shape=out_shape,
        compiler_params=_cparams(("parallel", "parallel")),
    )(x, tables, *weights)


def _attn_kernel(qe_ref, qo_ref, ke_ref, ko_ref, ve_ref, vo_ref, o_ref, acc_sc, *, tk):
    n_chunks = ve_ref.shape[0]
    tq = acc_sc.shape[2]
    refs = ((qe_ref, ke_ref, ve_ref), (qo_ref, ko_ref, vo_ref))
    qts = [jnp.transpose(q_ref[...].astype(F32)).astype(BF16) for q_ref, _, _ in refs]
    ones_rows = jnp.ones((ONES_ROWS, tk), BF16)

    def scores(j):
        start = pl.multiple_of(j * tk, tk)
        return [jnp.dot(k_ref[pl.ds(start, tk), :], qts[h], preferred_element_type=F32)
                for h, (_, k_ref, _) in enumerate(refs)]

    def weighted_values(h, j, p):
        v_ref = refs[h][2]
        lhs = jnp.concatenate([v_ref[j][h * HALF:(h + 1) * HALF], ones_rows], axis=0)
        return jnp.dot(lhs, p.astype(BF16), preferred_element_type=F32)

    def lagged_body(j, carry):
        sts = scores(j)
        out = []
        for h in range(2):
            ref_max, excess = carry[h]
            chunk_max = jnp.max(sts[h], axis=0, keepdims=True)
            p = jnp.exp2(sts[h] - ref_max)
            new_max = jnp.maximum(ref_max, chunk_max)
            acc_sc[h] = jnp.exp2(ref_max - new_max) * (acc_sc[h] + weighted_values(h, j, p))
            out.append((new_max, jnp.maximum(excess, chunk_max - ref_max)))
        return tuple(out)

    def exact_body(j, carry):
        sts = scores(j)
        out = []
        for h in range(2):
            new_max = jnp.maximum(carry[h], jnp.max(sts[h], axis=0, keepdims=True))
            p = jnp.exp2(sts[h] - new_max)
            acc_sc[h] = jnp.exp2(carry[h] - new_max) * acc_sc[h] + weighted_values(h, j, p)
            out.append(new_max)
        return tuple(out)

    def run(body, init):
        acc_sc[...] = jnp.zeros(acc_sc.shape, F32)
        carry = lax.fori_loop(0, n_chunks, body, init)
        ot = jnp.concatenate([acc_sc[h, :HALF] / acc_sc[h, HALF:HALF + 1] for h in range(2)],
                             axis=0)
        o_ref[...] = jnp.transpose(ot)
        return carry

    first_ref = [jnp.max(jnp.dot(k_ref[0:16, :], qts[h], preferred_element_type=F32),
                         axis=0, keepdims=True) for h, (_, k_ref, _) in enumerate(refs)]
    zero = jnp.zeros((1, tq), F32)
    (_, x_even), (_, x_odd) = run(lagged_body, tuple((r, zero) for r in first_ref))
    safe = jnp.max(jnp.maximum(x_even, x_odd)) < MAX_LAG_EXPONENT

    @pl.when(jnp.logical_not(safe))
    def _():
        neg_inf = jnp.full((1, tq), -jnp.inf, F32)
        run(exact_body, (neg_inf, neg_inf))


def _attention(q_arr, k_arr, vt_arr, maps, tq):
    b, _, s, _ = q_arr.shape
    n_pairs = 4
    n_chunks, _, tk = vt_arr.shape[2:]
    grid = (b, n_pairs, s // tq)
    qe, qo, ke, ko, ve, vo = maps

    def qspec(f):
        return pl.BlockSpec((None, None, tq, LANES), lambda bi, p, i: (bi, f(p), i, 0))

    def kspec(f):
        return pl.BlockSpec((None, None, s, LANES), lambda bi, p, i: (bi, f(p), 0, 0))

    def vspec(f):
        return pl.BlockSpec((None, None, n_chunks, LANES, tk),
                            lambda bi, p, i: (bi, f(p), 0, 0, 0))

    return pl.pallas_call(
        functools.partial(_attn_kernel, tk=tk), grid=grid,
        in_specs=[qspec(qe), qspec(qo), kspec(ke), kspec(ko), vspec(ve), vspec(vo)],
        out_specs=pl.BlockSpec((None, tq, LANES), lambda bi, p, i: (bi, i, p)),
        out_shape=jax.ShapeDtypeStruct((b, s, n_pairs * LANES), F32),
        scratch_shapes=[pltpu.VMEM((2, HALF + ONES_ROWS, tq), F32)],
        compiler_params=_cparams(("parallel", "parallel", "parallel")),
    )(q_arr, q_arr, k_arr, k_arr, vt_arr, vt_arr)


_GQA_MAPS = (lambda p: p, lambda p: p,
             lambda p: 2 * (p // 2), lambda p: 2 * (p // 2) + 1,
             lambda p: p // 2, lambda p: 1 - p // 2)
_MLA_MAPS = (lambda p: 2 * p, lambda p: 2 * p + 1,
             lambda p: 2 * p, lambda p: 2 * p + 1,
             lambda p: p, lambda p: p)


def _post_attn_kernel(x_ref, oa_ref, ob_ref, g_oa, g_ob, w_o, g_ffn, w_r,
                      x1_ref, h2_ref, afft_ref, aff_ref):
    na = _rms(oa_ref[...], g_oa[...]).astype(BF16)
    nb = _rms(ob_ref[...], g_ob[...]).astype(BF16)
    half = na.shape[1]
    mix = (jnp.dot(na, w_o[:half, :], preferred_element_type=F32)
           + jnp.dot(nb, w_o[half:, :], preferred_element_type=F32))
    x1 = x_ref[...] + mix
    x1_ref[...] = x1
    h2 = _rms(x1, g_ffn[...]).astype(BF16)
    h2_ref[...] = h2
    logits = jnp.dot(h2, w_r[...], preferred_element_type=F32)
    valid = lax.broadcasted_iota(jnp.int32, (1, LANES), 1) < N_EXPERTS
    logits = jnp.where(valid, logits, -jnp.inf)
    e = jnp.exp(logits - jnp.max(logits, axis=-1, keepdims=True))
    aff = e / jnp.sum(e, axis=-1, keepdims=True)
    afft_ref[...] = jnp.transpose(aff)[:N_EXPERTS, :]
    aff_ref[...] = aff[:, :N_EXPERTS]


def _post_attention(x, oa, ob, w, tm):
    b, s, d = x.shape
    nt = s // tm
    tok = lambda width: pl.BlockSpec((None, tm, width), lambda bi, i: (bi, i, 0))
    weights = [w['g_oa'], w['g_ob'], w['w_o'], w['g_ffn'], w['w_r']]
    return pl.pallas_call(
        _post_attn_kernel, grid=(b, nt),
        in_specs=[tok(d), tok(oa.shape[2]), tok(ob.shape[2])] + [_full(a.shape) for a in weights],
        out_specs=[tok(d), tok(d),
                   pl.BlockSpec((N_EXPERTS, tm), lambda bi, i: (0, bi * nt + i)),
                   tok(N_EXPERTS)],
        out_shape=[jax.ShapeDtypeStruct((b, s, d), F32), jax.ShapeDtypeStruct((b, s, d), BF16),
                   jax.ShapeDtypeStruct((N_EXPERTS, b * s), F32),
                   jax.ShapeDtypeStruct((b, s, N_EXPERTS), F32)],
        compiler_params=_cparams(("parallel", "parallel")),
    )(x, oa, ob, *weights)


SLOT_CHUNK = LANES
SUB_TILE = 256


def _threshold_kernel(aff_ref, thr_ref, need_ref, *, cap):
    def count(mask):
        return jnp.sum(jnp.where(mask, 1.0, 0.0), axis=1, keepdims=True)

    def body(i, prefix):
        cand = prefix | jnp.left_shift(jnp.int32(1), 30 - i)
        cnt = count(pltpu.bitcast(aff_ref[...], jnp.int32) >= cand)
        return jnp.where(cnt >= cap, cand, prefix)

    thr = lax.fori_loop(0, 31, body, jnp.zeros((N_EXPERTS, 1), jnp.int32))
    greater = count(pltpu.bitcast(aff_ref[...], jnp.int32) > thr)
    thr_ref[...] = thr
    need_ref[...] = (cap - greater).astype(jnp.int32)


def _position_kernel(thr_ref, need_ref, aff_ref, upper_ref, lower_ref, pos_ref):
    e = pl.program_id(0)
    bits = pltpu.bitcast(aff_ref[...], jnp.int32)
    thr = thr_ref[e]
    need = need_ref[e].astype(F32)
    upper = upper_ref[...]
    lower = lower_ref[...]

    def inclusive_cumsum(mask):
        m = jnp.where(mask, 1.0, 0.0).astype(BF16)
        within = jnp.dot(m, upper, preferred_element_type=F32)
        above = jnp.dot(lower, m, preferred_element_type=F32)
        return within + jnp.sum(above, axis=1, keepdims=True)

    greater = bits > thr
    equal = bits == thr
    equal_rank = inclusive_cumsum(equal) - jnp.where(equal, 1.0, 0.0)
    selected = greater | (equal & (equal_rank < need))
    pos = inclusive_cumsum(selected) - 1.0
    pos_ref[...] = jnp.where(selected, pos, -1.0).astype(jnp.int32)


def _select(afft, cap):
    e, n = afft.shape
    nb = n // LANES
    thr, need = pl.pallas_call(
        functools.partial(_threshold_kernel, cap=cap),
        out_shape=[jax.ShapeDtypeStruct((e, 1), jnp.int32)] * 2,
        compiler_params=pltpu.CompilerParams(vmem_limit_bytes=VMEM_LIMIT),
    )(afft)
    upper = jnp.triu(jnp.ones((LANES, LANES), F32)).astype(BF16)
    lower = jnp.tril(jnp.ones((nb, nb), F32), -1).astype(BF16)
    pos = pl.pallas_call(
        _position_kernel,
        grid_spec=pltpu.PrefetchScalarGridSpec(
            num_scalar_prefetch=2, grid=(e,),
            in_specs=[pl.BlockSpec((nb, LANES), lambda ei, *_: (ei, 0)),
                      pl.BlockSpec((LANES, LANES), lambda ei, *_: (0, 0)),
                      pl.BlockSpec((nb, nb), lambda ei, *_: (0, 0))],
            out_specs=pl.BlockSpec((nb, LANES), lambda ei, *_: (ei, 0))),
        out_shape=jax.ShapeDtypeStruct((e * nb, LANES), jnp.int32),
        compiler_params=_cparams(("parallel",)),
    )(thr.reshape(e), need.reshape(e), afft.reshape(e * nb, LANES), upper, lower)
    return pos.reshape(e, n)


def _tile_starts(pos, tile):
    e, n = pos.shape
    cnt = jnp.sum((pos >= 0).reshape(e, n // tile, tile), axis=-1, dtype=jnp.int32)
    return jnp.concatenate([jnp.zeros((e, 1), jnp.int32), jnp.cumsum(cnt, axis=1)], axis=1)


def _dispatch_kernel(even_ref, odd_ref, st_ref, pos_ref, h_ref, xe_even, xe_odd, *, nt, bs):
    e = pl.program_id(0)
    t = pl.program_id(1)
    i = e * nt + t
    prev = jnp.maximum(i - 1, 0)

    @pl.when((t == 0) | (even_ref[i] != even_ref[prev]))
    def _():
        xe_even[...] = jnp.zeros(xe_even.shape, xe_even.dtype)

    @pl.when((t == 0) | (odd_ref[i] != odd_ref[prev]))
    def _():
        xe_odd[...] = jnp.zeros(xe_odd.shape, xe_odd.dtype)

    n_sub = bs // SUB_TILE
    chunks_per_block = bs // SLOT_CHUNK
    base = e * (nt * n_sub + 1) + t * n_sub

    def gather(first_chunk, n_chunks, u):
        rows = n_chunks * SLOT_CHUNK
        slot = first_chunk * SLOT_CHUNK + lax.broadcasted_iota(jnp.int32, (rows, 1), 0)
        tokens = slice(u * SUB_TILE, (u + 1) * SUB_TILE)
        onehot = jnp.where(pos_ref[:, tokens] == slot, 1.0, 0.0).astype(BF16)
        return jnp.dot(onehot, h_ref[tokens, :], preferred_element_type=F32).astype(xe_even.dtype)

    def add_chunk(g, rows):
        off = pl.multiple_of((g % chunks_per_block) * SLOT_CHUNK, SLOT_CHUNK)
        even = (g // chunks_per_block) % 2 == 0
        zeros = jnp.zeros_like(rows)
        xe_even[pl.ds(off, SLOT_CHUNK), :] += jnp.where(even, rows, zeros)
        xe_odd[pl.ds(off, SLOT_CHUNK), :] += jnp.where(even, zeros, rows)

    for u in range(n_sub):
        g0 = st_ref[base + u] // SLOT_CHUNK
        val = gather(g0, 2, u)
        add_chunk(g0, val[:SLOT_CHUNK])
        add_chunk(g0 + 1, val[SLOT_CHUNK:])

        @pl.when(st_ref[base + u + 1] > (g0 + 2) * SLOT_CHUNK)
        def _(u=u, g0=g0):
            add_chunk(g0 + 2, gather(g0 + 2, 1, u))


def _dispatch(h2, pos, cap, bs):
    n, d = h2.shape
    e = pos.shape[0]
    tile = bs
    nt = n // tile
    nb = cap // bs
    assert cap % (2 * bs) == 0 and n % tile == 0 and bs % SUB_TILE == 0
    starts = _tile_starts(pos, SUB_TILE)
    b0 = starts[:, :-1:bs // SUB_TILE] // bs
    even_idx = ((b0 + 1) // 2).reshape(-1)
    odd_idx = (b0 // 2).reshape(-1)
    return pl.pallas_call(
        functools.partial(_dispatch_kernel, nt=nt, bs=bs),
        grid_spec=pltpu.PrefetchScalarGridSpec(
            num_scalar_prefetch=3, grid=(e, nt),
            in_specs=[pl.BlockSpec((None, 1, tile), lambda ei, t, *_: (ei, 0, t)),
                      pl.BlockSpec((tile, d), lambda ei, t, *_: (t, 0))],
            out_specs=[pl.BlockSpec((None, None, bs, d),
                                    lambda ei, t, ev, od, st: (ei, ev[ei * nt + t], 0, 0)),
                       pl.BlockSpec((None, None, bs, d),
                                    lambda ei, t, ev, od, st: (ei, od[ei * nt + t], 0, 0))]),
        out_shape=[jax.ShapeDtypeStruct((e, nb // 2 + 1, bs, d), BF16),
                   jax.ShapeDtypeStruct((e, nb // 2, bs, d), BF16)],
        compiler_params=_cparams(("arbitrary", "arbitrary")),
    )(even_idx, odd_idx, starts.reshape(-1), pos.reshape(e, 1, n), h2)


def _combine_kernel(st_ref, x1_ref, pos_ref, aff_ref, *refs, nt, by):
    y_ref = refs[-1]
    t = pl.program_id(0)

    def contribution(e, j):
        lo = (st_ref[e * (nt + 1) + t] // by + j) * by
        slot = lo + lax.broadcasted_iota(jnp.int32, (1, by), 1)
        onehot = jnp.where(pos_ref[:, e:e + 1] == slot, 1.0, 0.0).astype(BF16)
        part = jnp.dot(onehot, refs[2 * e + j][...], preferred_element_type=F32)
        return aff_ref[:, e:e + 1] * part

    acc = x1_ref[...]
    for e in range(N_EXPERTS):
        acc = acc + contribution(e, 0)
    y_ref[...] = acc
    for e in range(N_EXPERTS):
        spills_over = st_ref[e * (nt + 1) + t + 1] > (st_ref[e * (nt + 1) + t] // by + 1) * by

        @pl.when(spills_over)
        def _(e=e):
            y_ref[...] += contribution(e, 1)


def _combine(x1, pos_tok, aff_tok, ye, starts, tile):
    n, d = x1.shape
    e, cap, _ = ye.shape
    by = tile
    nt = n // tile
    nby = cap // by

    def ye_spec(ei, j):
        def index(t, st):
            return (ei, jnp.minimum(st[ei * (nt + 1) + t] // by + j, nby - 1), 0)
        return pl.BlockSpec((None, by, d), index)

    tok = lambda width: pl.BlockSpec((tile, width), lambda t, st: (t, 0))
    ye_specs = [ye_spec(ei, j) for ei in range(e) for j in range(2)]
    return pl.pallas_call(
        functools.partial(_combine_kernel, nt=nt, by=by),
        grid_spec=pltpu.PrefetchScalarGridSpec(
            num_scalar_prefetch=1, grid=(nt,),
            in_specs=[tok(d), tok(e), tok(e)] + ye_specs,
            out_specs=tok(d)),
        out_shape=jax.ShapeDtypeStruct((n, d), F32),
        compiler_params=_cparams(("arbitrary",)),
    )(starts.reshape(-1), x1, pos_tok, aff_tok, *([ye] * (2 * e)))


def _expert_kernel(xe_even, xe_odd, wg_ref, wu_ref, wd_ref, ye_ref):
    xe = jnp.where(pl.program_id(1) % 2 == 0, xe_even[...], xe_odd[...])
    a = jnp.dot(xe, wg_ref[...], preferred_element_type=F32)
    u = jnp.dot(xe, wu_ref[...], preferred_element_type=F32)
    hid = (a * jax.nn.sigmoid(a) * u).astype(BF16)
    ye_ref[...] = jnp.dot(hid, wd_ref[...], preferred_element_type=F32).astype(ye_ref.dtype)


def _expert_ffn(xe_even, xe_odd, w):
    e, nbh, bs, d = xe_odd.shape
    f = w['w_gate'].shape[2]
    xspec = pl.BlockSpec((None, None, bs, d), lambda ei, ci: (ei, ci // 2, 0, 0))
    wspec = lambda r, cc: pl.BlockSpec((None, r, cc), lambda ei, ci: (ei, 0, 0))
    return pl.pallas_call(
        _expert_kernel, grid=(e, 2 * nbh),
        in_specs=[xspec, xspec, wspec(d, f), wspec(d, f), wspec(f, d)],
        out_specs=pl.BlockSpec((None, bs, d), lambda ei, ci: (ei, ci, 0)),
        out_shape=jax.ShapeDtypeStruct((e, 2 * nbh * bs, d), BF16),
        compiler_params=_cparams(("parallel", "parallel")),
    )(xe_even, xe_odd, w['w_gate'], w['w_up'], w['w_down'])


def _tile(n, pref):
    t = min(n, pref)
    assert n % t == 0
    return t


def _encoder_layer(x, w):
    b, s, d = x.shape
    n = b * s
    tk = _tile(s, 1024)
    tables = _rope_tables(s)
    q, k, vt, qm, km, vmt = _pre_attention(x, tables, w, tk)
    tq = _tile(s, 2048)
    oa = _attention(q, k, vt, _GQA_MAPS, tq)
    ob = _attention(qm, km, vmt, _MLA_MAPS, tq)
    x1, h2, afft, aff_tok = _post_attention(x, oa, ob, w, _tile(s, 512))

    cap = CAPACITY_FACTOR * n // N_EXPERTS
    pos = _select(afft, cap)
    xe_even, xe_odd = _dispatch(h2.reshape(n, d), pos, cap, _tile(cap // 2, 1024))
    ye = _expert_ffn(xe_even, xe_odd, w)
    ctile = _tile(cap // 2, 256)
    y = _combine(x1.reshape(n, d), pos.T, aff_tok.reshape(n, N_EXPERTS), ye,
                 _tile_starts(pos, ctile), ctile)
    return y.reshape(b, s, d)


def kernel(x_prompt, x_sample, norm_attn, w_in, gqa_q_norm, gqa_k_norm, mla_q_a_norm, mla_w_uq,
           mla_q_nope_norm, mla_q_pe_norm, mla_kv_a_norm, mla_w_ukv, mla_k_nope_norm,
           mla_k_pe_norm, out_norm_a, out_norm_b, w_o, norm_ffn, w_router, w_gate, w_up, w_down):
    p = dict(norm_attn=norm_attn, w_in=w_in, gqa_q_norm=gqa_q_norm, gqa_k_norm=gqa_k_norm,
             mla_q_a_norm=mla_q_a_norm, mla_w_uq=mla_w_uq, mla_q_nope_norm=mla_q_nope_norm,
             mla_q_pe_norm=mla_q_pe_norm, mla_kv_a_norm=mla_kv_a_norm, mla_w_ukv=mla_w_ukv,
             mla_k_nope_norm=mla_k_nope_norm, mla_k_pe_norm=mla_k_pe_norm,
             out_norm_a=out_norm_a, out_norm_b=out_norm_b, w_o=w_o, norm_ffn=norm_ffn,
             w_router=w_router, w_gate=w_gate, w_up=w_up, w_down=w_down)
    w = _prep_weights(p)
    return (_encoder_layer(x_prompt, w), _encoder_layer(x_sample, w))
```

```python
import functools
import math

import jax
import jax.numpy as jnp
from jax import lax
from jax.experimental import pallas as pl
from jax.experimental.pallas import tpu as pltpu

D_MODEL = 1024
GRID_W = 64
ROPE_THETA = 10000.0
RMS_EPS = 1e-6
GQA_HEADS = 8
GQA_KV_HEADS = 2
GQA_HEAD_DIM = 64
MLA_HEADS = 8
MLA_Q_LORA = 256
MLA_KV_LORA = 128
MLA_NOPE = 64
MLA_ROPE = 32
MLA_V = 64
N_EXPERTS = 16
CAPACITY_FACTOR = 2
D_FF_EXPERT = 1024

LANES = 128
HALF = LANES // 2
D_IN_PAD = 1280
VMEM_LIMIT = 56 * 1024 * 1024
LOG2E = math.log2(math.e)
ONES_ROWS = 16
MAX_LAG_EXPONENT = 60.0

F32 = jnp.float32
BF16 = jnp.bfloat16


def _cparams(sem):
    return pltpu.CompilerParams(dimension_semantics=sem, vmem_limit_bytes=VMEM_LIMIT)


def _rope_tables(seq_len):
    rows = seq_len // GRID_W
    row = jnp.repeat(jnp.arange(rows, dtype=F32), GRID_W)
    col = jnp.tile(jnp.arange(GRID_W, dtype=F32), rows)

    def angles(rot_dim):
        half = rot_dim // 2
        inv = ROPE_THETA ** (-jnp.arange(0, half, 2, dtype=F32) / half)
        ar = row[:, None] * inv[None, :]
        ac = col[:, None] * inv[None, :]
        return jnp.concatenate([ar, ar, ac, ac], axis=-1)

    lane = jnp.arange(LANES)
    a64 = angles(GQA_HEAD_DIM)
    cos_a = jnp.tile(jnp.cos(a64), (1, 2))
    sin_a = jnp.tile(jnp.sin(a64), (1, 2))
    low_a = (lane % 32) < 16
    up_a = jnp.where(low_a, -sin_a, 0.0)
    dn_a = jnp.where(low_a, 0.0, sin_a)

    a32 = angles(MLA_ROPE)
    pad = ((0, 0), (MLA_NOPE, LANES - MLA_NOPE - MLA_ROPE))
    cos_b = jnp.pad(jnp.cos(a32) - 1.0, pad) + 1.0
    sin_b = jnp.pad(jnp.sin(a32), pad)
    low_b = ((lane - MLA_NOPE) % 16) < 8
    up_b = jnp.where(low_b, -sin_b, 0.0)
    dn_b = jnp.where(low_b, 0.0, sin_b)
    return jnp.stack([cos_a, up_a, dn_a, cos_b, up_b, dn_b])


def _segment_mean_matrix(groups):
    m = jnp.zeros((2 * LANES, 2 * LANES), F32)
    for base in (0, LANES):
        for start, width in groups:
            lo = base + start
            m = m.at[lo:lo + width, lo:lo + width].set(1.0 / width)
    return m.astype(BF16)


def _prep_weights(p):
    w_in = p['w_in']
    kpe_cols = jnp.pad(w_in[:, 1152:1184], ((0, 0), (MLA_NOPE, LANES - MLA_NOPE - MLA_ROPE)))
    w_in_p = jnp.concatenate([w_in[:, :1152], kpe_cols], axis=1).astype(BF16)
    w_uq = p['mla_w_uq'].reshape(MLA_Q_LORA, MLA_HEADS, MLA_NOPE + MLA_ROPE)
    w_uq_p = jnp.pad(w_uq, ((0, 0), (0, 0), (0, LANES - MLA_NOPE - MLA_ROPE)))
    w_uq_p = w_uq_p.reshape(MLA_Q_LORA, MLA_HEADS * LANES).astype(BF16)
    w_ukv = p['mla_w_ukv'].reshape(MLA_KV_LORA, MLA_HEADS, MLA_NOPE + MLA_V)
    w_uk_p = jnp.pad(w_ukv[:, :, :MLA_NOPE], ((0, 0), (0, 0), (0, LANES - MLA_NOPE)))
    w_uk_p = w_uk_p.reshape(MLA_KV_LORA, MLA_HEADS * LANES).astype(BF16)
    w_uv = w_ukv[:, :, MLA_NOPE:].reshape(MLA_KV_LORA, MLA_HEADS * MLA_V).astype(BF16)

    def row(v):
        return v.reshape(1, -1).astype(F32)

    zeros32 = jnp.zeros((LANES - MLA_NOPE - MLA_ROPE,), F32)
    zeros64 = jnp.zeros((HALF,), F32)
    return dict(
        w_in=w_in_p, w_uq=w_uq_p, w_uk=w_uk_p, w_uv=w_uv,
        g_attn=row(p['norm_attn']),
        g_q=row(jnp.tile(p['gqa_q_norm'], 2)),
        g_k=row(jnp.tile(p['gqa_k_norm'], 2)),
        g_cq=row(p['mla_q_a_norm']),
        g_qm=row(jnp.concatenate([p['mla_q_nope_norm'], p['mla_q_pe_norm'], zeros32])),
        g_ckv=row(p['mla_kv_a_norm']),
        g_kn=row(jnp.concatenate([p['mla_k_nope_norm'], zeros64])),
        g_kpe=row(jnp.concatenate([zeros64, p['mla_k_pe_norm'], zeros32])),
        s_gqa=_segment_mean_matrix([(0, HALF), (HALF, HALF)]),
        s_mla=_segment_mean_matrix([(0, MLA_NOPE), (MLA_NOPE, MLA_ROPE)]),
        g_oa=row(p['out_norm_a']), g_ob=row(p['out_norm_b']),
        w_o=p['w_o'].astype(BF16),
        g_ffn=row(p['norm_ffn']),
        w_r=jnp.pad(p['w_router'], ((0, 0), (0, LANES - N_EXPERTS))).astype(BF16),
        w_gate=p['w_gate'].astype(BF16), w_up=p['w_up'].astype(BF16),
        w_down=p['w_down'].astype(BF16),
    )


def _rms(x, gain):
    ms = jnp.mean(x * x, axis=-1, keepdims=True)
    return x * lax.rsqrt(ms + RMS_EPS) * gain


def _segment_inv_rms(x, seg_mat):
    width = seg_mat.shape[0]
    sq = (x * x).astype(BF16)
    ms = [jnp.dot(sq[:, c:c + width], seg_mat, preferred_element_type=F32)
          for c in range(0, x.shape[1], width)]
    return lax.rsqrt((ms[0] if len(ms) == 1 else jnp.concatenate(ms, axis=1)) + RMS_EPS)


def _rope(x, cos, up, dn, quarter):
    return (x * cos + pltpu.roll(x, LANES - quarter, 1) * up + pltpu.roll(x, quarter, 1) * dn)


def _pre_attn_kernel(x_ref, tab_ref, g_attn, w_in, g_q, g_k, g_cq, w_uq, g_qm, g_ckv, w_uk, w_uv,
                     g_kn, g_kpe, s_gqa, s_mla,
                     q_ref, k_ref, v_ref, qm_ref, km_ref, vm_ref):
    cos_a, up_a, dn_a = tab_ref[0], tab_ref[1], tab_ref[2]
    cos_b, up_b, dn_b = tab_ref[3], tab_ref[4], tab_ref[5]
    hn = _rms(x_ref[...], g_attn[...]).astype(BF16)
    proj = jnp.dot(hn, w_in[...], preferred_element_type=F32)

    lanes = lambda a, c: a[:, c * LANES:(c + 1) * LANES]
    sg = s_gqa[...]
    sm = s_mla[...]
    q_scale = LOG2E / math.sqrt(GQA_HEAD_DIM)
    gq = proj[:, :512]
    gq_inv = _segment_inv_rms(gq, sg)
    for c in range(GQA_HEADS // 2):
        xc = lanes(gq, c) * lanes(gq_inv, c) * g_q[...]
        q_ref[c] = (_rope(xc, cos_a, up_a, dn_a, 16) * q_scale).astype(BF16)

    low = lax.broadcasted_iota(jnp.int32, (1, LANES), 1) < HALF
    gk = proj[:, 512:640]
    k01 = _rope(gk * _segment_inv_rms(gk, s_gqa[:LANES, :LANES]) * g_k[...], cos_a, up_a, dn_a, 16)
    k10 = pltpu.roll(k01, HALF, 1)
    k_ref[0] = jnp.where(low, k01, 0.0).astype(BF16)
    k_ref[1] = jnp.where(low, 0.0, k10).astype(BF16)
    k_ref[2] = jnp.where(low, k10, 0.0).astype(BF16)
    k_ref[3] = jnp.where(low, 0.0, k01).astype(BF16)
    v01 = proj[:, 640:768]
    v_ref[0] = jnp.transpose(v01).astype(BF16)
    v_ref[1] = jnp.transpose(pltpu.roll(v01, HALF, 1)).astype(BF16)

    cq = _rms(proj[:, 768:1024], g_cq[...]).astype(BF16)
    qb = jnp.dot(cq, w_uq[...], preferred_element_type=F32)
    m_scale = LOG2E / math.sqrt(MLA_NOPE + MLA_ROPE)
    ckv = _rms(proj[:, 1024:1152], g_ckv[...]).astype(BF16)
    kn = jnp.dot(ckv, w_uk[...], preferred_element_type=F32)
    vv = jnp.dot(ckv, w_uv[...], preferred_element_type=F32)
    gkpe = proj[:, 1152:1280]
    kpe = _rope(gkpe * _segment_inv_rms(gkpe, s_mla[:LANES, :LANES]) * g_kpe[...],
                cos_b, up_b, dn_b, 8)
    qb_inv = _segment_inv_rms(qb, sm)
    kn_inv = _segment_inv_rms(kn, sm)
    for h in range(MLA_HEADS):
        qh = lanes(qb, h) * lanes(qb_inv, h) * g_qm[...]
        qm_ref[h] = (_rope(qh, cos_b, up_b, dn_b, 8) * m_scale).astype(BF16)
        km_ref[h] = (lanes(kn, h) * lanes(kn_inv, h) * g_kn[...] + kpe).astype(BF16)
    for c in range(MLA_HEADS // 2):
        vm_ref[c] = jnp.transpose(vv[:, c * LANES:(c + 1) * LANES]).astype(BF16)


def _full(shape):
    return pl.BlockSpec(shape, lambda *_: (0,) * len(shape))


def _pre_attention(x, tables, w, tm):
    b, s, d = x.shape
    nt = s // tm
    grid = (b, nt)

    def head_major(n):
        return pl.BlockSpec((None, n, tm, LANES), lambda bi, i: (bi, 0, i, 0))

    weights = [w['g_attn'], w['w_in'], w['g_q'], w['g_k'], w['g_cq'], w['w_uq'], w['g_qm'],
               w['g_ckv'], w['w_uk'], w['w_uv'], w['g_kn'], w['g_kpe'], w['s_gqa'], w['s_mla']]
    in_specs = [pl.BlockSpec((None, tm, d), lambda bi, i: (bi, i, 0)),
                pl.BlockSpec((6, tm, LANES), lambda bi, i: (0, i, 0))]
    in_specs += [_full(a.shape) for a in weights]
    def head_major_t(n):
        return pl.BlockSpec((None, n, None, LANES, tm), lambda bi, i: (bi, 0, i, 0, 0))

    counts = (4, 4, 2, 8, 8, 4)
    transposed = (False, False, True, False, False, True)
    out_shape = [jax.ShapeDtypeStruct((b, n, nt, LANES, tm) if t else (b, n, s, LANES), BF16)
                 for n, t in zip(counts, transposed)]
    out_specs = [head_major_t(n) if t else head_major(n) for n, t in zip(counts, transposed)]
    return pl.pallas_call(
        _pre_attn_kernel, grid=grid, in_specs=in_specs, out_specs=out_specs, out_shape=out_shape,
        compiler_params=_cparams(("parallel", "parallel")),
    )(x, tables, *weights)


def _attn_kernel(qe_ref, qo_ref, ke_ref, ko_ref, ve_ref, vo_ref, o_ref, acc_sc, *, tk):
    n_chunks = ve_ref.shape[0]
    tq = acc_sc.shape[2]
    refs = ((qe_ref, ke_ref, ve_ref), (qo_ref, ko_ref, vo_ref))
    qts = [jnp.transpose(q_ref[...].astype(F32)).astype(BF16) for q_ref, _, _ in refs]
    ones_rows = jnp.ones((ONES_ROWS, tk), BF16)

    def scores(j):
        start = pl.multiple_of(j * tk, tk)
        return [jnp.dot(k_ref[pl.ds(start, tk), :], qts[h], preferred_element_type=F32)
                for h, (_, k_ref, _) in enumerate(refs)]

    def weighted_values(h, j, p):
        v_ref = refs[h][2]
        lhs = jnp.concatenate([v_ref[j][h * HALF:(h + 1) * HALF], ones_rows], axis=0)
        return jnp.dot(lhs, p.astype(BF16), preferred_element_type=F32)

    def lagged_body(j, carry):
        sts = scores(j)
        out = []
        for h in range(2):
            ref_max, excess = carry[h]
            chunk_max = jnp.max(sts[h], axis=0, keepdims=True)
            p = jnp.exp2(sts[h] - ref_max)
            new_max = jnp.maximum(ref_max, chunk_max)
            acc_sc[h] = jnp.exp2(ref_max - new_max) * (acc_sc[h] + weighted_values(h, j, p))
            out.append((new_max, jnp.maximum(excess, chunk_max - ref_max)))
        return tuple(out)

    def exact_body(j, carry):
        sts = scores(j)
        out = []
        for h in range(2):
            new_max = jnp.maximum(carry[h], jnp.max(sts[h], axis=0, keepdims=True))
            p = jnp.exp2(sts[h] - new_max)
            acc_sc[h] = jnp.exp2(carry[h] - new_max) * acc_sc[h] + weighted_values(h, j, p)
            out.append(new_max)
        return tuple(out)

    def run(body, init):
        acc_sc[...] = jnp.zeros(acc_sc.shape, F32)
        carry = lax.fori_loop(0, n_chunks, body, init)
        ot = jnp.concatenate([acc_sc[h, :HALF] / acc_sc[h, HALF:HALF + 1] for h in range(2)],
                             axis=0)
        o_ref[...] = jnp.transpose(ot)
        return carry

    first_ref = [jnp.max(jnp.dot(k_ref[0:16, :], qts[h], preferred_element_type=F32),
                         axis=0, keepdims=True) for h, (_, k_ref, _) in enumerate(refs)]
    zero = jnp.zeros((1, tq), F32)
    (_, x_even), (_, x_odd) = run(lagged_body, tuple((r, zero) for r in first_ref))
    safe = jnp.max(jnp.maximum(x_even, x_odd)) < MAX_LAG_EXPONENT

    @pl.when(jnp.logical_not(safe))
    def _():
        neg_inf = jnp.full((1, tq), -jnp.inf, F32)
        run(exact_body, (neg_inf, neg_inf))


def _attention(q_arr, k_arr, vt_arr, maps, tq):
    b, _, s, _ = q_arr.shape
    n_pairs = 4
    n_chunks, _, tk = vt_arr.shape[2:]
    grid = (b, n_pairs, s // tq)
    qe, qo, ke, ko, ve, vo = maps

    def qspec(f):
        return pl.BlockSpec((None, None, tq, LANES), lambda bi, p, i: (bi, f(p), i, 0))

    def kspec(f):
        return pl.BlockSpec((None, None, s, LANES), lambda bi, p, i: (bi, f(p), 0, 0))

    def vspec(f):
        return pl.BlockSpec((None, None, n_chunks, LANES, tk),
                            lambda bi, p, i: (bi, f(p), 0, 0, 0))

    return pl.pallas_call(
        functools.partial(_attn_kernel, tk=tk), grid=grid,
        in_specs=[qspec(qe), qspec(qo), kspec(ke), kspec(ko), vspec(ve), vspec(vo)],
        out_specs=pl.BlockSpec((None, tq, LANES), lambda bi, p, i: (bi, i, p)),
        out_shape=jax.ShapeDtypeStruct((b, s, n_pairs * LANES), F32),
        scratch_shapes=[pltpu.VMEM((2, HALF + ONES_ROWS, tq), F32)],
        compiler_params=_cparams(("parallel", "parallel", "parallel")),
    )(q_arr, q_arr, k_arr, k_arr, vt_arr, vt_arr)


_GQA_MAPS = (lambda p: p, lambda p: p,
             lambda p: 2 * (p // 2), lambda p: 2 * (p // 2) + 1,
             lambda p: p // 2, lambda p: 1 - p // 2)
_MLA_MAPS = (lambda p: 2 * p, lambda p: 2 * p + 1,
             lambda p: 2 * p, lambda p: 2 * p + 1,
             lambda p: p, lambda p: p)


def _post_attn_kernel(x_ref, oa_ref, ob_ref, g_oa, g_ob, w_o, g_ffn, w_r,
                      x1_ref, h2_ref, afft_ref, aff_ref):
    na = _rms(oa_ref[...], g_oa[...]).astype(BF16)
    nb = _rms(ob_ref[...], g_ob[...]).astype(BF16)
    half = na.shape[1]
    mix = (jnp.dot(na, w_o[:half, :], preferred_element_type=F32)
           + jnp.dot(nb, w_o[half:, :], preferred_element_type=F32))
    x1 = x_ref[...] + mix
    x1_ref[...] = x1
    h2 = _rms(x1, g_ffn[...]).astype(BF16)
    h2_ref[...] = h2
    logits = jnp.dot(h2, w_r[...], preferred_element_type=F32)
    valid = lax.broadcasted_iota(jnp.int32, (1, LANES), 1) < N_EXPERTS
    logits = jnp.where(valid, logits, -jnp.inf)
    e = jnp.exp(logits - jnp.max(logits, axis=-1, keepdims=True))
    aff = e / jnp.sum(e, axis=-1, keepdims=True)
    afft_ref[...] = jnp.transpose(aff)[:N_EXPERTS, :]
    aff_ref[...] = aff[:, :N_EXPERTS]


def _post_attention(x, oa, ob, w, tm):
    b, s, d = x.shape
    nt = s // tm
    tok = lambda width: pl.BlockSpec((None, tm, width), lambda bi, i: (bi, i, 0))
    weights = [w['g_oa'], w['g_ob'], w['w_o'], w['g_ffn'], w['w_r']]
    return pl.pallas_call(
        _post_attn_kernel, grid=(b, nt),
        in_specs=[tok(d), tok(oa.shape[2]), tok(ob.shape[2])] + [_full(a.shape) for a in weights],
        out_specs=[tok(d), tok(d),
                   pl.BlockSpec((N_EXPERTS, tm), lambda bi, i: (0, bi * nt + i)),
                   tok(N_EXPERTS)],
        out_shape=[jax.ShapeDtypeStruct((b, s, d), F32), jax.ShapeDtypeStruct((b, s, d), BF16),
                   jax.ShapeDtypeStruct((N_EXPERTS, b * s), F32),
                   jax.ShapeDtypeStruct((b, s, N_EXPERTS), F32)],
        compiler_params=_cparams(("parallel", "parallel")),
    )(x, oa, ob, *weights)


SLOT_CHUNK = LANES
SUB_TILE = 256


def _threshold_kernel(aff_ref, thr_ref, need_ref, *, cap):
    def count(mask):
        return jnp.sum(jnp.where(mask, 1.0, 0.0), axis=1, keepdims=True)

    def body(i, prefix):
        cand = prefix | jnp.left_shift(jnp.int32(1), 30 - i)
        cnt = count(pltpu.bitcast(aff_ref[...], jnp.int32) >= cand)
        return jnp.where(cnt >= cap, cand, prefix)

    thr = lax.fori_loop(0, 31, body, jnp.zeros((N_EXPERTS, 1), jnp.int32))
    greater = count(pltpu.bitcast(aff_ref[...], jnp.int32) > thr)
    thr_ref[...] = thr
    need_ref[...] = (cap - greater).astype(jnp.int32)


def _position_kernel(thr_ref, need_ref, aff_ref, upper_ref, lower_ref, pos_ref):
    e = pl.program_id(0)
    bits = pltpu.bitcast(aff_ref[...], jnp.int32)
    thr = thr_ref[e]
    need = need_ref[e].astype(F32)
    upper = upper_ref[...]
    lower = lower_ref[...]

    def inclusive_cumsum(mask):
        m = jnp.where(mask, 1.0, 0.0).astype(BF16)
        within = jnp.dot(m, upper, preferred_element_type=F32)
        above = jnp.dot(lower, m, preferred_element_type=F32)
        return within + jnp.sum(above, axis=1, keepdims=True)

    greater = bits > thr
    equal = bits == thr
    equal_rank = inclusive_cumsum(equal) - jnp.where(equal, 1.0, 0.0)
    selected = greater | (equal & (equal_rank < need))
    pos = inclusive_cumsum(selected) - 1.0
    pos_ref[...] = jnp.where(selected, pos, -1.0).astype(jnp.int32)


def _select(afft, cap):
    e, n = afft.shape
    nb = n // LANES
    thr, need = pl.pallas_call(
        functools.partial(_threshold_kernel, cap=cap),
        out_shape=[jax.ShapeDtypeStruct((e, 1), jnp.int32)] * 2,
        compiler_params=pltpu.CompilerParams(vmem_limit_bytes=VMEM_LIMIT),
    )(afft)
    upper = jnp.triu(jnp.ones((LANES, LANES), F32)).astype(BF16)
    lower = jnp.tril(jnp.ones((nb, nb), F32), -1).astype(BF16)
    pos = pl.pallas_call(
        _position_kernel,
        grid_spec=pltpu.PrefetchScalarGridSpec(
            num_scalar_prefetch=2, grid=(e,),
            in_specs=[pl.BlockSpec((nb, LANES), lambda ei, *_: (ei, 0)),
                      pl.BlockSpec((LANES, LANES), lambda ei, *_: (0, 0)),
                      pl.BlockSpec((nb, nb), lambda ei, *_: (0, 0))],
            out_specs=pl.BlockSpec((nb, LANES), lambda ei, *_: (ei, 0))),
        out_shape=jax.ShapeDtypeStruct((e * nb, LANES), jnp.int32),
        compiler_params=_cparams(("parallel",)),
    )(thr.reshape(e), need.reshape(e), afft.reshape(e * nb, LANES), upper, lower)
    return pos.reshape(e, n)


def _tile_starts(pos, tile):
    e, n = pos.shape
    cnt = jnp.sum((pos >= 0).reshape(e, n // tile, tile), axis=-1, dtype=jnp.int32)
    return jnp.concatenate([jnp.zeros((e, 1), jnp.int32), jnp.cumsum(cnt, axis=1)], axis=1)


def _dispatch_kernel(even_ref, odd_ref, st_ref, pos_ref, h_ref, xe_even, xe_odd, *, nt, bs):
    e = pl.program_id(0)
    t = pl.program_id(1)
    i = e * nt + t
    prev = jnp.maximum(i - 1, 0)

    @pl.when((t == 0) | (even_ref[i] != even_ref[prev]))
    def _():
        xe_even[...] = jnp.zeros(xe_even.shape, xe_even.dtype)

    @pl.when((t == 0) | (odd_ref[i] != odd_ref[prev]))
    def _():
        xe_odd[...] = jnp.zeros(xe_odd.shape, xe_odd.dtype)

    n_sub = bs // SUB_TILE
    chunks_per_block = bs // SLOT_CHUNK
    base = e * (nt * n_sub + 1) + t * n_sub

    def gather(first_chunk, n_chunks, u):
        rows = n_chunks * SLOT_CHUNK
        slot = first_chunk * SLOT_CHUNK + lax.broadcasted_iota(jnp.int32, (rows, 1), 0)
        tokens = slice(u * SUB_TILE, (u + 1) * SUB_TILE)
        onehot = jnp.where(pos_ref[:, tokens] == slot, 1.0, 0.0).astype(BF16)
        return jnp.dot(onehot, h_ref[tokens, :], preferred_element_type=F32).astype(xe_even.dtype)

    def add_chunk(g, rows):
        off = pl.multiple_of((g % chunks_per_block) * SLOT_CHUNK, SLOT_CHUNK)
        even = (g // chunks_per_block) % 2 == 0
        zeros = jnp.zeros_like(rows)
        xe_even[pl.ds(off, SLOT_CHUNK), :] += jnp.where(even, rows, zeros)
        xe_odd[pl.ds(off, SLOT_CHUNK), :] += jnp.where(even, zeros, rows)

    for u in range(n_sub):
        g0 = st_ref[base + u] // SLOT_CHUNK
        val = gather(g0, 2, u)
        add_chunk(g0, val[:SLOT_CHUNK])
        add_chunk(g0 + 1, val[SLOT_CHUNK:])
    for u in range(n_sub):
        g0 = st_ref[base + u] // SLOT_CHUNK

        @pl.when(st_ref[base + u + 1] > (g0 + 2) * SLOT_CHUNK)
        def _(u=u, g0=g0):
            add_chunk(g0 + 2, gather(g0 + 2, 1, u))


def _dispatch(h2, pos, cap, bs):
    n, d = h2.shape
    e = pos.shape[0]
    tile = bs
    nt = n // tile
    nb = cap // bs
    assert cap % (2 * bs) == 0 and n % tile == 0 and bs % SUB_TILE == 0
    starts = _tile_starts(pos, SUB_TILE)
    b0 = starts[:, :-1:bs // SUB_TILE] // bs
    even_idx = ((b0 + 1) // 2).reshape(-1)
    odd_idx = (b0 // 2).reshape(-1)
    return pl.pallas_call(
        functools.partial(_dispatch_kernel, nt=nt, bs=bs),
        grid_spec=pltpu.PrefetchScalarGridSpec(
            num_scalar_prefetch=3, grid=(e, nt),
            in_specs=[pl.BlockSpec((None, 1, tile), lambda ei, t, *_: (ei, 0, t)),
                      pl.BlockSpec((tile, d), lambda ei, t, *_: (t, 0))],
            out_specs=[pl.BlockSpec((None, None, bs, d),
                                    lambda ei, t, ev, od, st: (ei, ev[ei * nt + t], 0, 0)),
                       pl.BlockSpec((None, None, bs, d),
                                    lambda ei, t, ev, od, st: (ei, od[ei * nt + t], 0, 0))]),
        out_shape=[jax.ShapeDtypeStruct((e, nb // 2 + 1, bs, d), BF16),
                   jax.ShapeDtypeStruct((e, nb // 2, bs, d), BF16)],
        compiler_params=_cparams(("arbitrary", "arbitrary")),
    )(even_idx, odd_idx, starts.reshape(-1), pos.reshape(e, 1, n), h2)


def _combine_kernel(st_ref, x1_ref, pos_ref, aff_ref, *refs, nt, by):
    y_ref = refs[-1]
    t = pl.program_id(0)

    def contribution(e, j):
        lo = (st_ref[e * (nt + 1) + t] // by + j) * by
        slot = lo + lax.broadcasted_iota(jnp.int32, (1, by), 1)
        onehot = jnp.where(pos_ref[:, e:e + 1] == slot, 1.0, 0.0).astype(BF16)
        part = jnp.dot(onehot, refs[2 * e + j][...], preferred_element_type=F32)
        return aff_ref[:, e:e + 1] * part

    acc = x1_ref[...]
    for e in range(N_EXPERTS):
        acc = acc + contribution(e, 0)
    y_ref[...] = acc
    for e in range(N_EXPERTS):
        spills_over = st_ref[e * (nt + 1) + t + 1] > (st_ref[e * (nt + 1) + t] // by + 1) * by

        @pl.when(spills_over)
        def _(e=e):
            y_ref[...] += contribution(e, 1)


def _combine(x1, pos_tok, aff_tok, ye, starts, tile):
    n, d = x1.shape
    e, cap, _ = ye.shape
    by = tile
    nt = n // tile
    nby = cap // by

    def ye_spec(ei, j):
        def index(t, st):
            return (ei, jnp.minimum(st[ei * (nt + 1) + t] // by + j, nby - 1), 0)
        return pl.BlockSpec((None, by, d), index)

    tok = lambda width: pl.BlockSpec((tile, width), lambda t, st: (t, 0))
    ye_specs = [ye_spec(ei, j) for ei in range(e) for j in range(2)]
    return pl.pallas_call(
        functools.partial(_combine_kernel, nt=nt, by=by),
        grid_spec=pltpu.PrefetchScalarGridSpec(
            num_scalar_prefetch=1, grid=(nt,),
            in_specs=[tok(d), tok(e), tok(e)] + ye_specs,
            out_specs=tok(d)),
        out_shape=jax.ShapeDtypeStruct((n, d), F32),
        compiler_params=_cparams(("arbitrary",)),
    )(starts.reshape(-1), x1, pos_tok, aff_tok, *([ye] * (2 * e)))


def _expert_kernel(xe_even, xe_odd, wg_ref, wu_ref, wd_ref, ye_ref):
    xe = jnp.where(pl.program_id(1) % 2 == 0, xe_even[...], xe_odd[...])
    a = jnp.dot(xe, wg_ref[...], preferred_element_type=F32)
    u = jnp.dot(xe, wu_ref[...], preferred_element_type=F32)
    hid = (a * jax.nn.sigmoid(a) * u).astype(BF16)
    ye_ref[...] = jnp.dot(hid, wd_ref[...], preferred_element_type=F32).astype(ye_ref.dtype)


def _expert_ffn(xe_even, xe_odd, w):
    e, nbh, bs, d = xe_odd.shape
    f = w['w_gate'].shape[2]
    xspec = pl.BlockSpec((None, None, bs, d), lambda ei, ci: (ei, ci // 2, 0, 0))
    wspec = lambda r, cc: pl.BlockSpec((None, r, cc), lambda ei, ci: (ei, 0, 0))
    return pl.pallas_call(
        _expert_kernel, grid=(e, 2 * nbh),
        in_specs=[xspec, xspec, wspec(d, f), wspec(d, f), wspec(f, d)],
        out_specs=pl.BlockSpec((None, bs, d), lambda ei, ci: (ei, ci, 0)),
        out_shape=jax.ShapeDtypeStruct((e, 2 * nbh * bs, d), BF16),
        compiler_params=_cparams(("parallel", "parallel")),
    )(xe_even, xe_odd, w['w_gate'], w['w_up'], w['w_down'])


def _tile(n, pref):
    t = min(n, pref)
    assert n % t == 0
    return t


def _encoder_layer(x, w):
    b, s, d = x.shape
    n = b * s
    tk = _tile(s, 1024)
    tables = _rope_tables(s)
    q, k, vt, qm, km, vmt = _pre_attention(x, tables, w, tk)
    tq = _tile(s, 4096)
    oa = _attention(q, k, vt, _GQA_MAPS, tq)
    ob = _attention(qm, km, vmt, _MLA_MAPS, tq)
    x1, h2, afft, aff_tok = _post_attention(x, oa, ob, w, _tile(s, 512))

    cap = CAPACITY_FACTOR * n // N_EXPERTS
    pos = _select(afft, cap)
    xe_even, xe_odd = _dispatch(h2.reshape(n, d), pos, cap, _tile(cap // 2, 1024))
    ye = _expert_ffn(xe_even, xe_odd, w)
    ctile = _tile(cap // 2, 256)
    y = _combine(x1.reshape(n, d), pos.T, aff_tok.reshape(n, N_EXPERTS), ye,
                 _tile_starts(pos, ctile), ctile)
    return y.reshape(b, s, d)


def kernel(x_prompt, x_sample, norm_attn, w_in, gqa_q_norm, gqa_k_norm, mla_q_a_norm, mla_w_uq,
           mla_q_nope_norm, mla_q_pe_norm, mla_kv_a_norm, mla_w_ukv, mla_k_nope_norm,
           mla_k_pe_norm, out_norm_a, out_norm_b, w_o, norm_ffn, w_router, w_gate, w_up, w_down):
    p = dict(norm_attn=norm_attn, w_in=w_in, gqa_q_norm=gqa_q_norm, gqa_k_norm=gqa_k_norm,
             mla_q_a_norm=mla_q_a_norm, mla_w_uq=mla_w_uq, mla_q_nope_norm=mla_q_nope_norm,
             mla_q_pe_norm=mla_q_pe_norm, mla_kv_a_norm=mla_kv_a_norm, mla_w_ukv=mla_w_ukv,
             mla_k_nope_norm=mla_k_nope_norm, mla_k_pe_norm=mla_k_pe_norm,
             out_norm_a=out_norm_a, out_norm_b=out_norm_b, w_o=w_o, norm_ffn=norm_ffn,
             w_router=w_router, w_gate=w_gate, w_up=w_up, w_down=w_down)
    w = _prep_weights(p)
    return (_encoder_layer(x_prompt, w), _encoder_layer(x_sample, w))
```

```python
import functools
import math

import jax
import jax.numpy as jnp
from jax import lax
from jax.experimental import pallas as pl
from jax.experimental.pallas import tpu as pltpu

D_MODEL = 1024
GRID_W = 64
ROPE_THETA = 10000.0
RMS_EPS = 1e-6
GQA_HEADS = 8
GQA_KV_HEADS = 2
GQA_HEAD_DIM = 64
MLA_HEADS = 8
MLA_Q_LORA = 256
MLA_KV_LORA = 128
MLA_NOPE = 64
MLA_ROPE = 32
MLA_V = 64
N_EXPERTS = 16
CAPACITY_FACTOR = 2
D_FF_EXPERT = 1024

LANES = 128
HALF = LANES // 2
D_IN_PAD = 1280
VMEM_LIMIT = 56 * 1024 * 1024
LOG2E = math.log2(math.e)
ONES_ROWS = 16
MAX_LAG_EXPONENT = 60.0

F32 = jnp.float32
BF16 = jnp.bfloat16


def _cparams(sem):
    return pltpu.CompilerParams(dimension_semantics=sem, vmem_limit_bytes=VMEM_LIMIT)


def _rope_tables(seq_len):
    rows = seq_len // GRID_W
    row = jnp.repeat(jnp.arange(rows, dtype=F32), GRID_W)
    col = jnp.tile(jnp.arange(GRID_W, dtype=F32), rows)

    def angles(rot_dim):
        half = rot_dim // 2
        inv = ROPE_THETA ** (-jnp.arange(0, half, 2, dtype=F32) / half)
        ar = row[:, None] * inv[None, :]
        ac = col[:, None] * inv[None, :]
        return jnp.concatenate([ar, ar, ac, ac], axis=-1)

    lane = jnp.arange(LANES)
    a64 = angles(GQA_HEAD_DIM)
    cos_a = jnp.tile(jnp.cos(a64), (1, 2))
    sin_a = jnp.tile(jnp.sin(a64), (1, 2))
    low_a = (lane % 32) < 16
    up_a = jnp.where(low_a, -sin_a, 0.0)
    dn_a = jnp.where(low_a, 0.0, sin_a)

    a32 = angles(MLA_ROPE)
    pad = ((0, 0), (MLA_NOPE, LANES - MLA_NOPE - MLA_ROPE))
    cos_b = jnp.pad(jnp.cos(a32) - 1.0, pad) + 1.0
    sin_b = jnp.pad(jnp.sin(a32), pad)
    low_b = ((lane - MLA_NOPE) % 16) < 8
    up_b = jnp.where(low_b, -sin_b, 0.0)
    dn_b = jnp.where(low_b, 0.0, sin_b)
    return jnp.stack([cos_a, up_a, dn_a, cos_b, up_b, dn_b])


def _segment_mean_matrix(groups):
    m = jnp.zeros((2 * LANES, 2 * LANES), F32)
    for base in (0, LANES):
        for start, width in groups:
            lo = base + start
            m = m.at[lo:lo + width, lo:lo + width].set(1.0 / width)
    return m.astype(BF16)


def _prep_weights(p):
    w_in = p['w_in']
    kpe_cols = jnp.pad(w_in[:, 1152:1184], ((0, 0), (MLA_NOPE, LANES - MLA_NOPE - MLA_ROPE)))
    w_in_p = jnp.concatenate([w_in[:, :1152], kpe_cols], axis=1).astype(BF16)
    w_uq = p['mla_w_uq'].reshape(MLA_Q_LORA, MLA_HEADS, MLA_NOPE + MLA_ROPE)
    w_uq_p = jnp.pad(w_uq, ((0, 0), (0, 0), (0, LANES - MLA_NOPE - MLA_ROPE)))
    w_uq_p = w_uq_p.reshape(MLA_Q_LORA, MLA_HEADS * LANES).astype(BF16)
    w_ukv = p['mla_w_ukv'].reshape(MLA_KV_LORA, MLA_HEADS, MLA_NOPE + MLA_V)
    w_uk_p = jnp.pad(w_ukv[:, :, :MLA_NOPE], ((0, 0), (0, 0), (0, LANES - MLA_NOPE)))
    w_uk_p = w_uk_p.reshape(MLA_KV_LORA, MLA_HEADS * LANES).astype(BF16)
    w_uv = w_ukv[:, :, MLA_NOPE:].reshape(MLA_KV_LORA, MLA_HEADS * MLA_V).astype(BF16)

    def row(v):
        return v.reshape(1, -1).astype(F32)

    zeros32 = jnp.zeros((LANES - MLA_NOPE - MLA_ROPE,), F32)
    zeros64 = jnp.zeros((HALF,), F32)
    return dict(
        w_in=w_in_p, w_uq=w_uq_p, w_uk=w_uk_p, w_uv=w_uv,
        g_attn=row(p['norm_attn']),
        g_q=row(jnp.tile(p['gqa_q_norm'], 2)),
        g_k=row(jnp.tile(p['gqa_k_norm'], 2)),
        g_cq=row(p['mla_q_a_norm']),
        g_qm=row(jnp.concatenate([p['mla_q_nope_norm'], p['mla_q_pe_norm'], zeros32])),
        g_ckv=row(p['mla_kv_a_norm']),
        g_kn=row(jnp.concatenate([p['mla_k_nope_norm'], zeros64])),
        g_kpe=row(jnp.concatenate([zeros64, p['mla_k_pe_norm'], zeros32])),
        s_gqa=_segment_mean_matrix([(0, HALF), (HALF, HALF)]),
        s_mla=_segment_mean_matrix([(0, MLA_NOPE), (MLA_NOPE, MLA_ROPE)]),
        g_oa=row(p['out_norm_a']), g_ob=row(p['out_norm_b']),
        w_o=p['w_o'].astype(BF16),
        g_ffn=row(p['norm_ffn']),
        w_r=jnp.pad(p['w_router'], ((0, 0), (0, LANES - N_EXPERTS))).astype(BF16),
        w_gate=p['w_gate'].astype(BF16), w_up=p['w_up'].astype(BF16),
        w_down=p['w_down'].astype(BF16),
    )


def _rms(x, gain):
    ms = jnp.mean(x * x, axis=-1, keepdims=True)
    return x * lax.rsqrt(ms + RMS_EPS) * gain


def _segment_inv_rms(x, seg_mat):
    width = seg_mat.shape[0]
    sq = (x * x).astype(BF16)
    ms = [jnp.dot(sq[:, c:c + width], seg_mat, preferred_element_type=F32)
          for c in range(0, x.shape[1], width)]
    return lax.rsqrt((ms[0] if len(ms) == 1 else jnp.concatenate(ms, axis=1)) + RMS_EPS)


def _rope(x, cos, up, dn, quarter):
    return (x * cos + pltpu.roll(x, LANES - quarter, 1) * up + pltpu.roll(x, quarter, 1) * dn)


def _pre_attn_kernel(x_ref, tab_ref, g_attn, w_in, g_q, g_k, g_cq, w_uq, g_qm, g_ckv, w_uk, w_uv,
                     g_kn, g_kpe, s_gqa, s_mla,
                     q_ref, k_ref, v_ref, qm_ref, km_ref, vm_ref):
    cos_a, up_a, dn_a = tab_ref[0], tab_ref[1], tab_ref[2]
    cos_b, up_b, dn_b = tab_ref[3], tab_ref[4], tab_ref[5]
    hn = _rms(x_ref[...], g_attn[...]).astype(BF16)
    proj = jnp.dot(hn, w_in[...], preferred_element_type=F32)

    lanes = lambda a, c: a[:, c * LANES:(c + 1) * LANES]
    sg = s_gqa[...]
    sm = s_mla[...]
    q_scale = LOG2E / math.sqrt(GQA_HEAD_DIM)
    gq = proj[:, :512]
    gq_inv = _segment_inv_rms(gq, sg)
    for c in range(GQA_HEADS // 2):
        xc = lanes(gq, c) * lanes(gq_inv, c) * g_q[...]
        q_ref[c] = (_rope(xc, cos_a, up_a, dn_a, 16) * q_scale).astype(BF16)

    low = lax.broadcasted_iota(jnp.int32, (1, LANES), 1) < HALF
    gk = proj[:, 512:640]
    k01 = _rope(gk * _segment_inv_rms(gk, s_gqa[:LANES, :LANES]) * g_k[...], cos_a, up_a, dn_a, 16)
    k10 = pltpu.roll(k01, HALF, 1)
    k_ref[0] = jnp.where(low, k01, 0.0).astype(BF16)
    k_ref[1] = jnp.where(low, 0.0, k10).astype(BF16)
    k_ref[2] = jnp.where(low, k10, 0.0).astype(BF16)
    k_ref[3] = jnp.where(low, 0.0, k01).astype(BF16)
    v01 = proj[:, 640:768]
    v_ref[0] = jnp.transpose(v01).astype(BF16)
    v_ref[1] = jnp.transpose(pltpu.roll(v01, HALF, 1)).astype(BF16)

    cq = _rms(proj[:, 768:1024], g_cq[...]).astype(BF16)
    qb = jnp.dot(cq, w_uq[...], preferred_element_type=F32)
    m_scale = LOG2E / math.sqrt(MLA_NOPE + MLA_ROPE)
    ckv = _rms(proj[:, 1024:1152], g_ckv[...]).astype(BF16)
    kn = jnp.dot(ckv, w_uk[...], preferred_element_type=F32)
    vv = jnp.dot(ckv, w_uv[...], preferred_element_type=F32)
    gkpe = proj[:, 1152:1280]
    kpe = _rope(gkpe * _segment_inv_rms(gkpe, s_mla[:LANES, :LANES]) * g_kpe[...],
                cos_b, up_b, dn_b, 8)
    qb_inv = _segment_inv_rms(qb, sm)
    kn_inv = _segment_inv_rms(kn, sm)
    for h in range(MLA_HEADS):
        qh = lanes(qb, h) * lanes(qb_inv, h) * g_qm[...]
        qm_ref[h] = (_rope(qh, cos_b, up_b, dn_b, 8) * m_scale).astype(BF16)
        km_ref[h] = (lanes(kn, h) * lanes(kn_inv, h) * g_kn[...] + kpe).astype(BF16)
    for c in range(MLA_HEADS // 2):
        vm_ref[c] = jnp.transpose(vv[:, c * LANES:(c + 1) * LANES]).astype(BF16)


def _full(shape):
    return pl.BlockSpec(shape, lambda *_: (0,) * len(shape))


def _pre_attention(x, tables, w, tm):
    b, s, d = x.shape
    nt = s // tm
    grid = (b, nt)

    def head_major(n):
        return pl.BlockSpec((None, n, tm, LANES), lambda bi, i: (bi, 0, i, 0))

    weights = [w['g_attn'], w['w_in'], w['g_q'], w['g_k'], w['g_cq'], w['w_uq'], w['g_qm'],
               w['g_ckv'], w['w_uk'], w['w_uv'], w['g_kn'], w['g_kpe'], w['s_gqa'], w['s_mla']]
    in_specs = [pl.BlockSpec((None, tm, d), lambda bi, i: (bi, i, 0)),
                pl.BlockSpec((6, tm, LANES), lambda bi, i: (0, i, 0))]
    in_specs += [_full(a.shape) for a in weights]
    def head_major_t(n):
        return pl.BlockSpec((None, n, None, LANES, tm), lambda bi, i: (bi, 0, i, 0, 0))

    counts = (4, 4, 2, 8, 8, 4)
    transposed = (False, False, True, False, False, True)
    out_shape = [jax.ShapeDtypeStruct((b, n, nt, LANES, tm) if t else (b, n, s, LANES), BF16)
                 for n, t in zip(counts, transposed)]
    out_specs = [head_major_t(n) if t else head_major(n) for n, t in zip(counts, transposed)]
    return pl.pallas_call(
        _pre_attn_kernel, grid=grid, in_specs=in_specs, out_specs=out_specs, out_shape=out_shape,
        compiler_params=_cparams(("parallel", "parallel")),
    )(x, tables, *weights)


def _attn_kernel(qe_ref, qo_ref, ke_ref, ko_ref, ve_ref, vo_ref, o_ref, acc_sc, *, tk):
    n_chunks = ve_ref.shape[0]
    tq = acc_sc.shape[2]
    refs = ((qe_ref, ke_ref, ve_ref), (qo_ref, ko_ref, vo_ref))
    qts = [jnp.transpose(q_ref[...].astype(F32)).astype(BF16) for q_ref, _, _ in refs]
    ones_rows = jnp.ones((ONES_ROWS, tk), BF16)

    def scores(j):
        start = pl.multiple_of(j * tk, tk)
        return [jnp.dot(k_ref[pl.ds(start, tk), :], qts[h], preferred_element_type=F32)
                for h, (_, k_ref, _) in enumerate(refs)]

    def weighted_values(h, j, p):
        v_ref = refs[h][2]
        lhs = jnp.concatenate([v_ref[j][h * HALF:(h + 1) * HALF], ones_rows], axis=0)
        return jnp.dot(lhs, p.astype(BF16), preferred_element_type=F32)

    def lagged_body(j, carry):
        sts = scores(j)
        out = []
        for h in range(2):
            ref_max, excess = carry[h]
            chunk_max = jnp.max(sts[h], axis=0, keepdims=True)
            p = jnp.exp2(sts[h] - ref_max)
            new_max = jnp.maximum(ref_max, chunk_max)
            acc_sc[h] = jnp.exp2(ref_max - new_max) * (acc_sc[h] + weighted_values(h, j, p))
            out.append((new_max, jnp.maximum(excess, chunk_max - ref_max)))
        return tuple(out)

    def exact_body(j, carry):
        sts = scores(j)
        out = []
        for h in range(2):
            new_max = jnp.maximum(carry[h], jnp.max(sts[h], axis=0, keepdims=True))
            p = jnp.exp2(sts[h] - new_max)
            acc_sc[h] = jnp.exp2(carry[h] - new_max) * acc_sc[h] + weighted_values(h, j, p)
            out.append(new_max)
        return tuple(out)

    def run(body, init):
        acc_sc[...] = jnp.zeros(acc_sc.shape, F32)
        carry = lax.fori_loop(0, n_chunks, body, init)
        ot = jnp.concatenate([acc_sc[h, :HALF] / acc_sc[h, HALF:HALF + 1] for h in range(2)],
                             axis=0)
        o_ref[...] = jnp.transpose(ot)
        return carry

    first_ref = [jnp.max(jnp.dot(k_ref[0:16, :], qts[h], preferred_element_type=F32),
                         axis=0, keepdims=True) for h, (_, k_ref, _) in enumerate(refs)]
    zero = jnp.zeros((1, tq), F32)
    (_, x_even), (_, x_odd) = run(lagged_body, tuple((r, zero) for r in first_ref))
    safe = jnp.max(jnp.maximum(x_even, x_odd)) < MAX_LAG_EXPONENT

    @pl.when(jnp.logical_not(safe))
    def _():
        neg_inf = jnp.full((1, tq), -jnp.inf, F32)
        run(exact_body, (neg_inf, neg_inf))


def _attention(q_arr, k_arr, vt_arr, maps, tq):
    b, _, s, _ = q_arr.shape
    n_pairs = 4
    n_chunks, _, tk = vt_arr.shape[2:]
    grid = (b, n_pairs, s // tq)
    qe, qo, ke, ko, ve, vo = maps

    def qspec(f):
        return pl.BlockSpec((None, None, tq, LANES), lambda bi, p, i: (bi, f(p), i, 0))

    def kspec(f):
        return pl.BlockSpec((None, None, s, LANES), lambda bi, p, i: (bi, f(p), 0, 0))

    def vspec(f):
        return pl.BlockSpec((None, None, n_chunks, LANES, tk),
                            lambda bi, p, i: (bi, f(p), 0, 0, 0))

    return pl.pallas_call(
        functools.partial(_attn_kernel, tk=tk), grid=grid,
        in_specs=[qspec(qe), qspec(qo), kspec(ke), kspec(ko), vspec(ve), vspec(vo)],
        out_specs=pl.BlockSpec((None, tq, LANES), lambda bi, p, i: (bi, i, p)),
        out_shape=jax.ShapeDtypeStruct((b, s, n_pairs * LANES), F32),
        scratch_shapes=[pltpu.VMEM((2, HALF + ONES_ROWS, tq), F32)],
        compiler_params=_cparams(("parallel", "parallel", "parallel")),
    )(q_arr, q_arr, k_arr, k_arr, vt_arr, vt_arr)


_GQA_MAPS = (lambda p: p, lambda p: p,
             lambda p: 2 * (p // 2), lambda p: 2 * (p // 2) + 1,
             lambda p: p // 2, lambda p: 1 - p // 2)
_MLA_MAPS = (lambda p: 2 * p, lambda p: 2 * p + 1,
             lambda p: 2 * p, lambda p: 2 * p + 1,
             lambda p: p, lambda p: p)


def _post_attn_kernel(x_ref, oa_ref, ob_ref, g_oa, g_ob, w_o, g_ffn, w_r,
                      x1_ref, h2_ref, afft_ref, aff_ref):
    na = _rms(oa_ref[...], g_oa[...]).astype(BF16)
    nb = _rms(ob_ref[...], g_ob[...]).astype(BF16)
    half = na.shape[1]
    mix = (jnp.dot(na, w_o[:half, :], preferred_element_type=F32)
           + jnp.dot(nb, w_o[half:, :], preferred_element_type=F32))
    x1 = x_ref[...] + mix
    x1_ref[...] = x1
    h2 = _rms(x1, g_ffn[...]).astype(BF16)
    h2_ref[...] = h2
    logits = jnp.dot(h2, w_r[...], preferred_element_type=F32)
    valid = lax.broadcasted_iota(jnp.int32, (1, LANES), 1) < N_EXPERTS
    logits = jnp.where(valid, logits, -jnp.inf)
    e = jnp.exp(logits - jnp.max(logits, axis=-1, keepdims=True))
    aff = e / jnp.sum(e, axis=-1, keepdims=True)
    afft_ref[...] = jnp.transpose(aff)[:N_EXPERTS, :]
    aff_ref[...] = aff[:, :N_EXPERTS]


def _post_attention(x, oa, ob, w, tm):
    b, s, d = x.shape
    nt = s // tm
    tok = lambda width: pl.BlockSpec((None, tm, width), lambda bi, i: (bi, i, 0))
    weights = [w['g_oa'], w['g_ob'], w['w_o'], w['g_ffn'], w['w_r']]
    return pl.pallas_call(
        _post_attn_kernel, grid=(b, nt),
        in_specs=[tok(d), tok(oa.shape[2]), tok(ob.shape[2])] + [_full(a.shape) for a in weights],
        out_specs=[tok(d), tok(d),
                   pl.BlockSpec((N_EXPERTS, tm), lambda bi, i: (0, bi * nt + i)),
                   tok(N_EXPERTS)],
        out_shape=[jax.ShapeDtypeStruct((b, s, d), F32), jax.ShapeDtypeStruct((b, s, d), BF16),
                   jax.ShapeDtypeStruct((N_EXPERTS, b * s), F32),
                   jax.ShapeDtypeStruct((b, s, N_EXPERTS), F32)],
        compiler_params=_cparams(("parallel", "parallel")),
    )(x, oa, ob, *weights)


MXU_DEPTH = 256
SLOT_CHUNK = LANES
SUB_TILE = MXU_DEPTH
PACKED_ROWS = 16
COMBINE_WINDOW = 64


def _threshold_kernel(aff_ref, thr_ref, need_ref, *, cap):
    def count(mask):
        return jnp.sum(jnp.where(mask, 1.0, 0.0), axis=1, keepdims=True)

    def body(i, prefix):
        cand = prefix | jnp.left_shift(jnp.int32(1), 30 - i)
        cnt = count(pltpu.bitcast(aff_ref[...], jnp.int32) >= cand)
        return jnp.where(cnt >= cap, cand, prefix)

    thr = lax.fori_loop(0, 31, body, jnp.zeros((N_EXPERTS, 1), jnp.int32))
    greater = count(pltpu.bitcast(aff_ref[...], jnp.int32) > thr)
    thr_ref[...] = thr
    need_ref[...] = (cap - greater).astype(jnp.int32)


def _position_kernel(thr_ref, need_ref, aff_ref, upper_ref, lower_ref, pos_ref):
    e = pl.program_id(0)
    bits = pltpu.bitcast(aff_ref[...], jnp.int32)
    thr = thr_ref[e]
    need = need_ref[e].astype(F32)
    upper = upper_ref[...]
    lower = lower_ref[...]

    def inclusive_cumsum(mask):
        m = jnp.where(mask, 1.0, 0.0).astype(BF16)
        within = jnp.dot(m, upper, preferred_element_type=F32)
        above = jnp.dot(lower, m, preferred_element_type=F32)
        return within + jnp.sum(above, axis=1, keepdims=True)

    greater = bits > thr
    equal = bits == thr
    equal_rank = inclusive_cumsum(equal) - jnp.where(equal, 1.0, 0.0)
    selected = greater | (equal & (equal_rank < need))
    pos = inclusive_cumsum(selected) - 1.0
    pos_ref[...] = jnp.where(selected, pos, -1.0).astype(jnp.int32)


def _select(afft, cap):
    e, n = afft.shape
    nb = n // LANES
    thr, need = pl.pallas_call(
        functools.partial(_threshold_kernel, cap=cap),
        out_shape=[jax.ShapeDtypeStruct((e, 1), jnp.int32)] * 2,
        compiler_params=pltpu.CompilerParams(vmem_limit_bytes=VMEM_LIMIT),
    )(afft)
    upper = jnp.triu(jnp.ones((LANES, LANES), F32)).astype(BF16)
    lower = jnp.tril(jnp.ones((nb, nb), F32), -1).astype(BF16)
    pos = pl.pallas_call(
        _position_kernel,
        grid_spec=pltpu.PrefetchScalarGridSpec(
            num_scalar_prefetch=2, grid=(e,),
            in_specs=[pl.BlockSpec((nb, LANES), lambda ei, *_: (ei, 0)),
                      pl.BlockSpec((LANES, LANES), lambda ei, *_: (0, 0)),
                      pl.BlockSpec((nb, nb), lambda ei, *_: (0, 0))],
            out_specs=pl.BlockSpec((nb, LANES), lambda ei, *_: (ei, 0))),
        out_shape=jax.ShapeDtypeStruct((e * nb, LANES), jnp.int32),
        compiler_params=_cparams(("parallel",)),
    )(thr.reshape(e), need.reshape(e), afft.reshape(e * nb, LANES), upper, lower)
    return pos.reshape(e, n)


def _tile_starts(pos, tile):
    e, n = pos.shape
    cnt = jnp.sum((pos >= 0).reshape(e, n // tile, tile), axis=-1, dtype=jnp.int32)
    return jnp.concatenate([jnp.zeros((e, 1), jnp.int32), jnp.cumsum(cnt, axis=1)], axis=1)


def _dispatch_kernel(even_ref, odd_ref, st_ref, pos_ref, h_ref, xe_even, xe_odd, *, nt, bs):
    e = pl.program_id(0)
    t = pl.program_id(1)
    i = e * nt + t
    prev = jnp.maximum(i - 1, 0)

    @pl.when((t == 0) | (even_ref[i] != even_ref[prev]))
    def _():
        xe_even[...] = jnp.zeros(xe_even.shape, xe_even.dtype)

    @pl.when((t == 0) | (odd_ref[i] != odd_ref[prev]))
    def _():
        xe_odd[...] = jnp.zeros(xe_odd.shape, xe_odd.dtype)

    n_sub = bs // SUB_TILE
    chunks_per_block = bs // SLOT_CHUNK
    base = e * (nt * n_sub + 1) + t * n_sub

    def gather(first_chunk, n_chunks, u):
        rows = n_chunks * SLOT_CHUNK
        slot = first_chunk * SLOT_CHUNK + lax.broadcasted_iota(jnp.int32, (rows, 1), 0)
        tokens = slice(u * SUB_TILE, (u + 1) * SUB_TILE)
        onehot = jnp.where(pos_ref[:, tokens] == slot, 1.0, 0.0).astype(BF16)
        return jnp.dot(onehot, h_ref[tokens, :], preferred_element_type=F32).astype(xe_even.dtype)

    def add_chunk(g, rows):
        off = pl.multiple_of((g % chunks_per_block) * SLOT_CHUNK, SLOT_CHUNK)
        even = (g // chunks_per_block) % 2 == 0
        zeros = jnp.zeros_like(rows)
        xe_even[pl.ds(off, SLOT_CHUNK), :] += jnp.where(even, rows, zeros)
        xe_odd[pl.ds(off, SLOT_CHUNK), :] += jnp.where(even, zeros, rows)

    for u in range(n_sub):
        g0 = st_ref[base + u] // SLOT_CHUNK
        val = gather(g0, 2, u)
        add_chunk(g0, val[:SLOT_CHUNK])
        add_chunk(g0 + 1, val[SLOT_CHUNK:])
    for u in range(n_sub):
        g0 = st_ref[base + u] // SLOT_CHUNK

        @pl.when(st_ref[base + u + 1] > (g0 + 2) * SLOT_CHUNK)
        def _(u=u, g0=g0):
            add_chunk(g0 + 2, gather(g0 + 2, 1, u))


def _dispatch(h2, pos, cap, bs):
    n, d = h2.shape
    e = pos.shape[0]
    tile = bs
    nt = n // tile
    nb = cap // bs
    assert cap % (2 * bs) == 0 and n % tile == 0 and bs % SUB_TILE == 0
    starts = _tile_starts(pos, SUB_TILE)
    b0 = starts[:, :-1:bs // SUB_TILE] // bs
    even_idx = ((b0 + 1) // 2).reshape(-1)
    odd_idx = (b0 // 2).reshape(-1)
    return pl.pallas_call(
        functools.partial(_dispatch_kernel, nt=nt, bs=bs),
        grid_spec=pltpu.PrefetchScalarGridSpec(
            num_scalar_prefetch=3, grid=(e, nt),
            in_specs=[pl.BlockSpec((None, 1, tile), lambda ei, t, *_: (ei, 0, t)),
                      pl.BlockSpec((tile, d), lambda ei, t, *_: (t, 0))],
            out_specs=[pl.BlockSpec((None, None, bs, d),
                                    lambda ei, t, ev, od, st: (ei, ev[ei * nt + t], 0, 0)),
                       pl.BlockSpec((None, None, bs, d),
                                    lambda ei, t, ev, od, st: (ei, od[ei * nt + t], 0, 0))]),
        out_shape=[jax.ShapeDtypeStruct((e, nb // 2 + 1, bs, d), BF16),
                   jax.ShapeDtypeStruct((e, nb // 2, bs, d), BF16)],
        compiler_params=_cparams(("arbitrary", "arbitrary")),
    )(even_idx, odd_idx, starts.reshape(-1), pos.reshape(e, 1, n), h2)


def _combine_kernel(st_ref, x1_ref, pos_ref, aff_ref, *refs, nt, by):
    y_ref, stage_ref = refs[-2], refs[-1]
    t = pl.program_id(0)
    first = [st_ref[e * (nt + 1) + t] for e in range(N_EXPERTS)]
    last = [st_ref[e * (nt + 1) + t + 1] for e in range(N_EXPERTS)]
    window = [(s // PACKED_ROWS) * PACKED_ROWS for s in first]
    fits = last[0] - window[0] <= COMBINE_WINDOW
    for e in range(1, N_EXPERTS):
        fits = fits & (last[e] - window[e] <= COMBINE_WINDOW)

    @pl.when(fits)
    def _():
        per_group = stage_ref.shape[1] // COMBINE_WINDOW
        lane = lax.broadcasted_iota(jnp.int32, (1, stage_ref.shape[1]), 1)
        acc = x1_ref[...]
        for g in range(N_EXPERTS // per_group):
            weights = jnp.zeros((x1_ref.shape[0], stage_ref.shape[1]), F32)
            for j in range(per_group):
                e = g * per_group + j
                blk_a, blk_b = refs[2 * e], refs[2 * e + 1]
                rel = window[e] - (first[e] // by) * by
                for q in range(COMBINE_WINDOW // PACKED_ROWS):
                    r = rel + q * PACKED_ROWS
                    ra = pl.multiple_of(jnp.minimum(r, by - PACKED_ROWS), PACKED_ROWS)
                    rb = pl.multiple_of(jnp.clip(r - by, 0, by - PACKED_ROWS), PACKED_ROWS)
                    rows = jnp.where(r < by, blk_a[pl.ds(ra, PACKED_ROWS), :],
                                     blk_b[pl.ds(rb, PACKED_ROWS), :])
                    dst = j * COMBINE_WINDOW + q * PACKED_ROWS
                    stage_ref[g, dst:dst + PACKED_ROWS, :] = rows
                offset = pos_ref[:, e:e + 1] - window[e]
                inside = (offset >= 0) & (offset < COMBINE_WINDOW)
                target = jnp.where(inside, offset + j * COMBINE_WINDOW, -1)
                weights = jnp.where(lane == target, aff_ref[:, e:e + 1], weights)
            acc = acc + jnp.dot(weights.astype(BF16), stage_ref[g],
                                preferred_element_type=F32)
        y_ref[...] = acc

    def contribution(e, j):
        lo = (first[e] // by + j) * by
        slot = lo + lax.broadcasted_iota(jnp.int32, (1, by), 1)
        onehot = jnp.where(pos_ref[:, e:e + 1] == slot, 1.0, 0.0).astype(BF16)
        part = jnp.dot(onehot, refs[2 * e + j][...], preferred_element_type=F32)
        return aff_ref[:, e:e + 1] * part

    @pl.when(jnp.logical_not(fits))
    def _():
        acc = x1_ref[...]
        for e in range(N_EXPERTS):
            acc = acc + contribution(e, 0)
        y_ref[...] = acc
        for e in range(N_EXPERTS):
            @pl.when(last[e] > (first[e] // by + 1) * by)
            def _(e=e):
                y_ref[...] += contribution(e, 1)


def _combine(x1, pos_tok, aff_tok, ye, starts, tile):
    n, d = x1.shape
    e, cap, _ = ye.shape
    by = tile
    nt = n // tile
    nby = cap // by

    def ye_spec(ei, j):
        def index(t, st):
            return (ei, jnp.minimum(st[ei * (nt + 1) + t] // by + j, nby - 1), 0)
        return pl.BlockSpec((None, by, d), index)

    tok = lambda width: pl.BlockSpec((tile, width), lambda t, st: (t, 0))
    ye_specs = [ye_spec(ei, j) for ei in range(e) for j in range(2)]
    return pl.pallas_call(
        functools.partial(_combine_kernel, nt=nt, by=by),
        grid_spec=pltpu.PrefetchScalarGridSpec(
            num_scalar_prefetch=1, grid=(nt,),
            in_specs=[tok(d), tok(e), tok(e)] + ye_specs,
            out_specs=tok(d),
            scratch_shapes=[pltpu.VMEM((e * COMBINE_WINDOW // MXU_DEPTH, MXU_DEPTH, d), BF16)]),
        out_shape=jax.ShapeDtypeStruct((n, d), F32),
        compiler_params=_cparams(("arbitrary",)),
    )(starts.reshape(-1), x1, pos_tok, aff_tok, *([ye] * (2 * e)))


def _expert_kernel(xe_even, xe_odd, wg_ref, wu_ref, wd_ref, ye_ref):
    xe = jnp.where(pl.program_id(1) % 2 == 0, xe_even[...], xe_odd[...])
    a = jnp.dot(xe, wg_ref[...], preferred_element_type=F32)
    u = jnp.dot(xe, wu_ref[...], preferred_element_type=F32)
    hid = (a * jax.nn.sigmoid(a) * u).astype(BF16)
    ye_ref[...] = jnp.dot(hid, wd_ref[...], preferred_element_type=F32).astype(ye_ref.dtype)


def _expert_ffn(xe_even, xe_odd, w):
    e, nbh, bs, d = xe_odd.shape
    f = w['w_gate'].shape[2]
    xspec = pl.BlockSpec((None, None, bs, d), lambda ei, ci: (ei, ci // 2, 0, 0))
    wspec = lambda r, cc: pl.BlockSpec((None, r, cc), lambda ei, ci: (ei, 0, 0))
    return pl.pallas_call(
        _expert_kernel, grid=(e, 2 * nbh),
        in_specs=[xspec, xspec, wspec(d, f), wspec(d, f), wspec(f, d)],
        out_specs=pl.BlockSpec((None, bs, d), lambda ei, ci: (ei, ci, 0)),
        out_shape=jax.ShapeDtypeStruct((e, 2 * nbh * bs, d), BF16),
        compiler_params=_cparams(("parallel", "parallel")),
    )(xe_even, xe_odd, w['w_gate'], w['w_up'], w['w_down'])


def _tile(n, pref):
    t = min(n, pref)
    assert n % t == 0
    return t


def _encoder_layer(x, w):
    b, s, d = x.shape
    n = b * s
    tk = _tile(s, 1024)
    tables = _rope_tables(s)
    q, k, vt, qm, km, vmt = _pre_attention(x, tables, w, tk)
    tq = _tile(s, 4096)
    oa = _attention(q, k, vt, _GQA_MAPS, tq)
    ob = _attention(qm, km, vmt, _MLA_MAPS, tq)
    x1, h2, afft, aff_tok = _post_attention(x, oa, ob, w, _tile(s, 512))

    cap = CAPACITY_FACTOR * n // N_EXPERTS
    pos = _select(afft, cap)
    xe_even, xe_odd = _dispatch(h2.reshape(n, d), pos, cap, _tile(cap // 2, 1024))
    ye = _expert_ffn(xe_even, xe_odd, w)
    ctile = _tile(cap // 2, 256)
    y = _combine(x1.reshape(n, d), pos.T, aff_tok.reshape(n, N_EXPERTS), ye,
                 _tile_starts(pos, ctile), ctile)
    return y.reshape(b, s, d)


def kernel(x_prompt, x_sample, norm_attn, w_in, gqa_q_norm, gqa_k_norm, mla_q_a_norm, mla_w_uq,
           mla_q_nope_norm, mla_q_pe_norm, mla_kv_a_norm, mla_w_ukv, mla_k_nope_norm,
           mla_k_pe_norm, out_norm_a, out_norm_b, w_o, norm_ffn, w_router, w_gate, w_up, w_down):
    p = dict(norm_attn=norm_attn, w_in=w_in, gqa_q_norm=gqa_q_norm, gqa_k_norm=gqa_k_norm,
             mla_q_a_norm=mla_q_a_norm, mla_w_uq=mla_w_uq, mla_q_nope_norm=mla_q_nope_norm,
             mla_q_pe_norm=mla_q_pe_norm, mla_kv_a_norm=mla_kv_a_norm, mla_w_ukv=mla_w_ukv,
             mla_k_nope_norm=mla_k_nope_norm, mla_k_pe_norm=mla_k_pe_norm,
             out_norm_a=out_norm_a, out_norm_b=out_norm_b, w_o=w_o, norm_ffn=norm_ffn,
             w_router=w_router, w_gate=w_gate, w_up=w_up, w_down=w_down)
    w = _prep_weights(p)
    return (_encoder_layer(x_prompt, w), _encoder_layer(x_sample, w))
```

```python
import functools
import math

import jax
import jax.numpy as jnp
from jax import lax
from jax.experimental import pallas as pl
from jax.experimental.pallas import tpu as pltpu

D_MODEL = 1024
GRID_W = 64
ROPE_THETA = 10000.0
RMS_EPS = 1e-6
GQA_HEADS = 8
GQA_KV_HEADS = 2
GQA_HEAD_DIM = 64
MLA_HEADS = 8
MLA_Q_LORA = 256
MLA_KV_LORA = 128
MLA_NOPE = 64
MLA_ROPE = 32
MLA_V = 64
N_EXPERTS = 16
CAPACITY_FACTOR = 2
D_FF_EXPERT = 1024

LANES = 128
HALF = LANES // 2
D_IN_PAD = 1280
VMEM_LIMIT = 56 * 1024 * 1024
LOG2E = math.log2(math.e)
ONES_ROWS = 16
MAX_LAG_EXPONENT = 60.0

F32 = jnp.float32
BF16 = jnp.bfloat16


def _cparams(sem):
    return pltpu.CompilerParams(dimension_semantics=sem, vmem_limit_bytes=VMEM_LIMIT)


def _rope_tables(seq_len):
    rows = seq_len // GRID_W
    row = jnp.repeat(jnp.arange(rows, dtype=F32), GRID_W)
    col = jnp.tile(jnp.arange(GRID_W, dtype=F32), rows)

    def angles(rot_dim):
        half = rot_dim // 2
        inv = ROPE_THETA ** (-jnp.arange(0, half, 2, dtype=F32) / half)
        ar = row[:, None] * inv[None, :]
        ac = col[:, None] * inv[None, :]
        return jnp.concatenate([ar, ar, ac, ac], axis=-1)

    lane = jnp.arange(LANES)
    a64 = angles(GQA_HEAD_DIM)
    cos_a = jnp.tile(jnp.cos(a64), (1, 2))
    sin_a = jnp.tile(jnp.sin(a64), (1, 2))
    low_a = (lane % 32) < 16
    up_a = jnp.where(low_a, -sin_a, 0.0)
    dn_a = jnp.where(low_a, 0.0, sin_a)

    a32 = angles(MLA_ROPE)
    pad = ((0, 0), (MLA_NOPE, LANES - MLA_NOPE - MLA_ROPE))
    cos_b = jnp.pad(jnp.cos(a32) - 1.0, pad) + 1.0
    sin_b = jnp.pad(jnp.sin(a32), pad)
    low_b = ((lane - MLA_NOPE) % 16) < 8
    up_b = jnp.where(low_b, -sin_b, 0.0)
    dn_b = jnp.where(low_b, 0.0, sin_b)
    return jnp.stack([cos_a, up_a, dn_a, cos_b, up_b, dn_b])


def _segment_mean_matrix(groups):
    m = jnp.zeros((2 * LANES, 2 * LANES), F32)
    for base in (0, LANES):
        for start, width in groups:
            lo = base + start
            m = m.at[lo:lo + width, lo:lo + width].set(1.0 / width)
    return m.astype(BF16)


def _prep_weights(p):
    w_in = p['w_in']
    kpe_cols = jnp.pad(w_in[:, 1152:1184], ((0, 0), (MLA_NOPE, LANES - MLA_NOPE - MLA_ROPE)))
    w_in_p = jnp.concatenate([w_in[:, :1152], kpe_cols], axis=1).astype(BF16)
    w_uq = p['mla_w_uq'].reshape(MLA_Q_LORA, MLA_HEADS, MLA_NOPE + MLA_ROPE)
    w_uq_p = jnp.pad(w_uq, ((0, 0), (0, 0), (0, LANES - MLA_NOPE - MLA_ROPE)))
    w_uq_p = w_uq_p.reshape(MLA_Q_LORA, MLA_HEADS * LANES).astype(BF16)
    w_ukv = p['mla_w_ukv'].reshape(MLA_KV_LORA, MLA_HEADS, MLA_NOPE + MLA_V)
    w_uk_p = jnp.pad(w_ukv[:, :, :MLA_NOPE], ((0, 0), (0, 0), (0, LANES - MLA_NOPE)))
    w_uk_p = w_uk_p.reshape(MLA_KV_LORA, MLA_HEADS * LANES).astype(BF16)
    w_uv = w_ukv[:, :, MLA_NOPE:].reshape(MLA_KV_LORA, MLA_HEADS * MLA_V).astype(BF16)

    def row(v):
        return v.reshape(1, -1).astype(F32)

    zeros32 = jnp.zeros((LANES - MLA_NOPE - MLA_ROPE,), F32)
    zeros64 = jnp.zeros((HALF,), F32)
    return dict(
        w_in=w_in_p, w_uq=w_uq_p, w_uk=w_uk_p, w_uv=w_uv,
        g_attn=row(p['norm_attn']),
        g_q=row(jnp.tile(p['gqa_q_norm'], 2)),
        g_k=row(jnp.tile(p['gqa_k_norm'], 2)),
        g_cq=row(p['mla_q_a_norm']),
        g_qm=row(jnp.concatenate([p['mla_q_nope_norm'], p['mla_q_pe_norm'], zeros32])),
        g_ckv=row(p['mla_kv_a_norm']),
        g_kn=row(jnp.concatenate([p['mla_k_nope_norm'], zeros64])),
        g_kpe=row(jnp.concatenate([zeros64, p['mla_k_pe_norm'], zeros32])),
        s_gqa=_segment_mean_matrix([(0, HALF), (HALF, HALF)]),
        s_mla=_segment_mean_matrix([(0, MLA_NOPE), (MLA_NOPE, MLA_ROPE)]),
        g_oa=row(p['out_norm_a']), g_ob=row(p['out_norm_b']),
        w_o=p['w_o'].astype(BF16),
        g_ffn=row(p['norm_ffn']),
        w_r=jnp.pad(p['w_router'], ((0, 0), (0, LANES - N_EXPERTS))).astype(BF16),
        w_gate=p['w_gate'].astype(BF16), w_up=p['w_up'].astype(BF16),
        w_down=p['w_down'].astype(BF16),
    )


def _rms(x, gain):
    ms = jnp.mean(x * x, axis=-1, keepdims=True)
    return x * lax.rsqrt(ms + RMS_EPS) * gain


def _segment_inv_rms(x, seg_mat):
    width = seg_mat.shape[0]
    sq = (x * x).astype(BF16)
    ms = [jnp.dot(sq[:, c:c + width], seg_mat, preferred_element_type=F32)
          for c in range(0, x.shape[1], width)]
    return lax.rsqrt((ms[0] if len(ms) == 1 else jnp.concatenate(ms, axis=1)) + RMS_EPS)


def _rope(x, cos, up, dn, quarter):
    return (x * cos + pltpu.roll(x, LANES - quarter, 1) * up + pltpu.roll(x, quarter, 1) * dn)


def _pre_attn_kernel(x_ref, tab_ref, g_attn, w_in, g_q, g_k, g_cq, w_uq, g_qm, g_ckv, w_uk, w_uv,
                     g_kn, g_kpe, s_gqa, s_mla,
                     q_ref, k_ref, v_ref, qm_ref, km_ref, vm_ref):
    cos_a, up_a, dn_a = tab_ref[0], tab_ref[1], tab_ref[2]
    cos_b, up_b, dn_b = tab_ref[3], tab_ref[4], tab_ref[5]
    hn = _rms(x_ref[...], g_attn[...]).astype(BF16)
    proj = jnp.dot(hn, w_in[...], preferred_element_type=F32)

    lanes = lambda a, c: a[:, c * LANES:(c + 1) * LANES]
    sg = s_gqa[...]
    sm = s_mla[...]
    q_scale = LOG2E / math.sqrt(GQA_HEAD_DIM)
    gq = proj[:, :512]
    gq_inv = _segment_inv_rms(gq, sg)
    for c in range(GQA_HEADS // 2):
        xc = lanes(gq, c) * lanes(gq_inv, c) * g_q[...]
        q_ref[c] = (_rope(xc, cos_a, up_a, dn_a, 16) * q_scale).astype(BF16)

    low = lax.broadcasted_iota(jnp.int32, (1, LANES), 1) < HALF
    gk = proj[:, 512:640]
    k01 = _rope(gk * _segment_inv_rms(gk, s_gqa[:LANES, :LANES]) * g_k[...], cos_a, up_a, dn_a, 16)
    k10 = pltpu.roll(k01, HALF, 1)
    k_ref[0] = jnp.where(low, k01, 0.0).astype(BF16)
    k_ref[1] = jnp.where(low, 0.0, k10).astype(BF16)
    k_ref[2] = jnp.where(low, k10, 0.0).astype(BF16)
    k_ref[3] = jnp.where(low, 0.0, k01).astype(BF16)
    v01 = proj[:, 640:768]
    v_ref[0] = jnp.transpose(v01).astype(BF16)
    v_ref[1] = jnp.transpose(pltpu.roll(v01, HALF, 1)).astype(BF16)

    cq = _rms(proj[:, 768:1024], g_cq[...]).astype(BF16)
    qb = jnp.dot(cq, w_uq[...], preferred_element_type=F32)
    m_scale = LOG2E / math.sqrt(MLA_NOPE + MLA_ROPE)
    ckv = _rms(proj[:, 1024:1152], g_ckv[...]).astype(BF16)
    kn = jnp.dot(ckv, w_uk[...], preferred_element_type=F32)
    vv = jnp.dot(ckv, w_uv[...], preferred_element_type=F32)
    gkpe = proj[:, 1152:1280]
    kpe = _rope(gkpe * _segment_inv_rms(gkpe, s_mla[:LANES, :LANES]) * g_kpe[...],
                cos_b, up_b, dn_b, 8)
    qb_inv = _segment_inv_rms(qb, sm)
    kn_inv = _segment_inv_rms(kn, sm)
    for h in range(MLA_HEADS):
        qh = lanes(qb, h) * lanes(qb_inv, h) * g_qm[...]
        qm_ref[h] = (_rope(qh, cos_b, up_b, dn_b, 8) * m_scale).astype(BF16)
        km_ref[h] = (lanes(kn, h) * lanes(kn_inv, h) * g_kn[...] + kpe).astype(BF16)
    for c in range(MLA_HEADS // 2):
        vm_ref[c] = jnp.transpose(vv[:, c * LANES:(c + 1) * LANES]).astype(BF16)


def _full(shape):
    return pl.BlockSpec(shape, lambda *_: (0,) * len(shape))


def _pre_attention(x, tables, w, tm):
    b, s, d = x.shape
    nt = s // tm
    grid = (b, nt)

    def head_major(n):
        return pl.BlockSpec((None, n, tm, LANES), lambda bi, i: (bi, 0, i, 0))

    weights = [w['g_attn'], w['w_in'], w['g_q'], w['g_k'], w['g_cq'], w['w_uq'], w['g_qm'],
               w['g_ckv'], w['w_uk'], w['w_uv'], w['g_kn'], w['g_kpe'], w['s_gqa'], w['s_mla']]
    in_specs = [pl.BlockSpec((None, tm, d), lambda bi, i: (bi, i, 0)),
                pl.BlockSpec((6, tm, LANES), lambda bi, i: (0, i, 0))]
    in_specs += [_full(a.shape) for a in weights]
    def head_major_t(n):
        return pl.BlockSpec((None, n, None, LANES, tm), lambda bi, i: (bi, 0, i, 0, 0))

    counts = (4, 4, 2, 8, 8, 4)
    transposed = (False, False, True, False, False, True)
    out_shape = [jax.ShapeDtypeStruct((b, n, nt, LANES, tm) if t else (b, n, s, LANES), BF16)
                 for n, t in zip(counts, transposed)]
    out_specs = [head_major_t(n) if t else head_major(n) for n, t in zip(counts, transposed)]
    return pl.pallas_call(
        _pre_attn_kernel, grid=grid, in_specs=in_specs, out_specs=out_specs, out_shape=out_shape,
        compiler_params=_cparams(("parallel", "parallel")),
    )(x, tables, *weights)


def _attn_kernel(qe_ref, qo_ref, ke_ref, ko_ref, ve_ref, vo_ref, o_ref, acc_sc, *, tk):
    n_chunks = ve_ref.shape[0]
    tq = acc_sc.shape[2]
    refs = ((qe_ref, ke_ref, ve_ref), (qo_ref, ko_ref, vo_ref))
    qts = [jnp.transpose(q_ref[...].astype(F32)).astype(BF16) for q_ref, _, _ in refs]
    ones_rows = jnp.ones((ONES_ROWS, tk), BF16)

    def scores(j):
        start = pl.multiple_of(j * tk, tk)
        return [jnp.dot(k_ref[pl.ds(start, tk), :], qts[h], preferred_element_type=F32)
                for h, (_, k_ref, _) in enumerate(refs)]

    def weighted_values(h, j, p):
        v_ref = refs[h][2]
        lhs = jnp.concatenate([v_ref[j][h * HALF:(h + 1) * HALF], ones_rows], axis=0)
        return jnp.dot(lhs, p.astype(BF16), preferred_element_type=F32)

    def lagged_body(j, carry):
        sts = scores(j)
        out = []
        for h in range(2):
            ref_max, excess = carry[h]
            chunk_max = jnp.max(sts[h], axis=0, keepdims=True)
            p = jnp.exp2(sts[h] - ref_max)
            new_max = jnp.maximum(ref_max, chunk_max)
            acc_sc[h] = jnp.exp2(ref_max - new_max) * (acc_sc[h] + weighted_values(h, j, p))
            out.append((new_max, jnp.maximum(excess, chunk_max - ref_max)))
        return tuple(out)

    def exact_body(j, carry):
        sts = scores(j)
        out = []
        for h in range(2):
            new_max = jnp.maximum(carry[h], jnp.max(sts[h], axis=0, keepdims=True))
            p = jnp.exp2(sts[h] - new_max)
            acc_sc[h] = jnp.exp2(carry[h] - new_max) * acc_sc[h] + weighted_values(h, j, p)
            out.append(new_max)
        return tuple(out)

    def run(body, init):
        acc_sc[...] = jnp.zeros(acc_sc.shape, F32)
        carry = lax.fori_loop(0, n_chunks, body, init)
        ot = jnp.concatenate([acc_sc[h, :HALF] / acc_sc[h, HALF:HALF + 1] for h in range(2)],
                             axis=0)
        o_ref[...] = jnp.transpose(ot)
        return carry

    first_ref = [jnp.max(jnp.dot(k_ref[0:16, :], qts[h], preferred_element_type=F32),
                         axis=0, keepdims=True) for h, (_, k_ref, _) in enumerate(refs)]
    zero = jnp.zeros((1, tq), F32)
    (_, x_even), (_, x_odd) = run(lagged_body, tuple((r, zero) for r in first_ref))
    safe = jnp.max(jnp.maximum(x_even, x_odd)) < MAX_LAG_EXPONENT

    @pl.when(jnp.logical_not(safe))
    def _():
        neg_inf = jnp.full((1, tq), -jnp.inf, F32)
        run(exact_body, (neg_inf, neg_inf))


def _attention(q_arr, k_arr, vt_arr, maps, tq):
    b, _, s, _ = q_arr.shape
    n_pairs = 4
    n_chunks, _, tk = vt_arr.shape[2:]
    grid = (b, n_pairs, s // tq)
    qe, qo, ke, ko, ve, vo = maps

    def qspec(f):
        return pl.BlockSpec((None, None, tq, LANES), lambda bi, p, i: (bi, f(p), i, 0))

    def kspec(f):
        return pl.BlockSpec((None, None, s, LANES), lambda bi, p, i: (bi, f(p), 0, 0))

    def vspec(f):
        return pl.BlockSpec((None, None, n_chunks, LANES, tk),
                            lambda bi, p, i: (bi, f(p), 0, 0, 0))

    return pl.pallas_call(
        functools.partial(_attn_kernel, tk=tk), grid=grid,
        in_specs=[qspec(qe), qspec(qo), kspec(ke), kspec(ko), vspec(ve), vspec(vo)],
        out_specs=pl.BlockSpec((None, tq, LANES), lambda bi, p, i: (bi, i, p)),
        out_shape=jax.ShapeDtypeStruct((b, s, n_pairs * LANES), F32),
        scratch_shapes=[pltpu.VMEM((2, HALF + ONES_ROWS, tq), F32)],
        compiler_params=_cparams(("parallel", "parallel", "parallel")),
    )(q_arr, q_arr, k_arr, k_arr, vt_arr, vt_arr)


_GQA_MAPS = (lambda p: p, lambda p: p,
             lambda p: 2 * (p // 2), lambda p: 2 * (p // 2) + 1,
             lambda p: p // 2, lambda p: 1 - p // 2)
_MLA_MAPS = (lambda p: 2 * p, lambda p: 2 * p + 1,
             lambda p: 2 * p, lambda p: 2 * p + 1,
             lambda p: p, lambda p: p)


def _post_attn_kernel(x_ref, oa_ref, ob_ref, g_oa, g_ob, w_o, g_ffn, w_r,
                      x1_ref, h2_ref, afft_ref, aff_ref):
    na = _rms(oa_ref[...], g_oa[...]).astype(BF16)
    nb = _rms(ob_ref[...], g_ob[...]).astype(BF16)
    half = na.shape[1]
    mix = (jnp.dot(na, w_o[:half, :], preferred_element_type=F32)
           + jnp.dot(nb, w_o[half:, :], preferred_element_type=F32))
    x1 = x_ref[...] + mix
    x1_ref[...] = x1
    h2 = _rms(x1, g_ffn[...]).astype(BF16)
    h2_ref[...] = h2
    logits = jnp.dot(h2, w_r[...], preferred_element_type=F32)
    valid = lax.broadcasted_iota(jnp.int32, (1, LANES), 1) < N_EXPERTS
    logits = jnp.where(valid, logits, -jnp.inf)
    e = jnp.exp(logits - jnp.max(logits, axis=-1, keepdims=True))
    aff = e / jnp.sum(e, axis=-1, keepdims=True)
    afft_ref[...] = jnp.transpose(aff)[:N_EXPERTS, :]
    aff_ref[...] = aff[:, :N_EXPERTS]


def _post_attention(x, oa, ob, w, tm):
    b, s, d = x.shape
    nt = s // tm
    tok = lambda width: pl.BlockSpec((None, tm, width), lambda bi, i: (bi, i, 0))
    weights = [w['g_oa'], w['g_ob'], w['w_o'], w['g_ffn'], w['w_r']]
    return pl.pallas_call(
        _post_attn_kernel, grid=(b, nt),
        in_specs=[tok(d), tok(oa.shape[2]), tok(ob.shape[2])] + [_full(a.shape) for a in weights],
        out_specs=[tok(d), tok(d),
                   pl.BlockSpec((N_EXPERTS, tm), lambda bi, i: (0, bi * nt + i)),
                   tok(N_EXPERTS)],
        out_shape=[jax.ShapeDtypeStruct((b, s, d), F32), jax.ShapeDtypeStruct((b, s, d), BF16),
                   jax.ShapeDtypeStruct((N_EXPERTS, b * s), F32),
                   jax.ShapeDtypeStruct((b, s, N_EXPERTS), F32)],
        compiler_params=_cparams(("parallel", "parallel")),
    )(x, oa, ob, *weights)


MXU_DEPTH = 256
SLOT_CHUNK = LANES
SUB_TILE = MXU_DEPTH
PACKED_ROWS = 16
COMBINE_WINDOW = 64
DISPATCH_WINDOW = 64
DISPATCH_GROUP = 2


def _threshold_kernel(aff_ref, thr_ref, need_ref, *, cap):
    def count(mask):
        return jnp.sum(jnp.where(mask, 1.0, 0.0), axis=1, keepdims=True)

    def body(i, prefix):
        cand = prefix | jnp.left_shift(jnp.int32(1), 30 - i)
        cnt = count(pltpu.bitcast(aff_ref[...], jnp.int32) >= cand)
        return jnp.where(cnt >= cap, cand, prefix)

    thr = lax.fori_loop(0, 31, body, jnp.zeros((N_EXPERTS, 1), jnp.int32))
    greater = count(pltpu.bitcast(aff_ref[...], jnp.int32) > thr)
    thr_ref[...] = thr
    need_ref[...] = (cap - greater).astype(jnp.int32)


def _position_kernel(thr_ref, need_ref, aff_ref, upper_ref, lower_ref, pos_ref):
    e = pl.program_id(0)
    bits = pltpu.bitcast(aff_ref[...], jnp.int32)
    thr = thr_ref[e]
    need = need_ref[e].astype(F32)
    upper = upper_ref[...]
    lower = lower_ref[...]

    def inclusive_cumsum(mask):
        m = jnp.where(mask, 1.0, 0.0).astype(BF16)
        within = jnp.dot(m, upper, preferred_element_type=F32)
        above = jnp.dot(lower, m, preferred_element_type=F32)
        return within + jnp.sum(above, axis=1, keepdims=True)

    greater = bits > thr
    equal = bits == thr
    equal_rank = inclusive_cumsum(equal) - jnp.where(equal, 1.0, 0.0)
    selected = greater | (equal & (equal_rank < need))
    pos = inclusive_cumsum(selected) - 1.0
    pos_ref[...] = jnp.where(selected, pos, -1.0).astype(jnp.int32)


def _select(afft, cap):
    e, n = afft.shape
    nb = n // LANES
    thr, need = pl.pallas_call(
        functools.partial(_threshold_kernel, cap=cap),
        out_shape=[jax.ShapeDtypeStruct((e, 1), jnp.int32)] * 2,
        compiler_params=pltpu.CompilerParams(vmem_limit_bytes=VMEM_LIMIT),
    )(afft)
    upper = jnp.triu(jnp.ones((LANES, LANES), F32)).astype(BF16)
    lower = jnp.tril(jnp.ones((nb, nb), F32), -1).astype(BF16)
    pos = pl.pallas_call(
        _position_kernel,
        grid_spec=pltpu.PrefetchScalarGridSpec(
            num_scalar_prefetch=2, grid=(e,),
            in_specs=[pl.BlockSpec((nb, LANES), lambda ei, *_: (ei, 0)),
                      pl.BlockSpec((LANES, LANES), lambda ei, *_: (0, 0)),
                      pl.BlockSpec((nb, nb), lambda ei, *_: (0, 0))],
            out_specs=pl.BlockSpec((nb, LANES), lambda ei, *_: (ei, 0))),
        out_shape=jax.ShapeDtypeStruct((e * nb, LANES), jnp.int32),
        compiler_params=_cparams(("parallel",)),
    )(thr.reshape(e), need.reshape(e), afft.reshape(e * nb, LANES), upper, lower)
    return pos.reshape(e, n)


def _tile_starts(pos, tile):
    e, n = pos.shape
    cnt = jnp.sum((pos >= 0).reshape(e, n // tile, tile), axis=-1, dtype=jnp.int32)
    return jnp.concatenate([jnp.zeros((e, 1), jnp.int32), jnp.cumsum(cnt, axis=1)], axis=1)


def _dispatch_kernel(even_ref, odd_ref, st_ref, pos_ref, h_ref, *outs, nt, bs):
    for x in range(DISPATCH_GROUP):
        _dispatch_expert(pl.program_id(0) * DISPATCH_GROUP + x, even_ref, odd_ref, st_ref,
                         pos_ref.at[x], h_ref, outs[2 * x], outs[2 * x + 1], nt=nt, bs=bs)


def _dispatch_expert(e, even_ref, odd_ref, st_ref, pos_ref, h_ref, xe_even, xe_odd, *, nt, bs):
    t = pl.program_id(1)
    i = e * nt + t
    prev = jnp.maximum(i - 1, 0)

    @pl.when((t == 0) | (even_ref[i] != even_ref[prev]))
    def _():
        xe_even[...] = jnp.zeros(xe_even.shape, xe_even.dtype)

    @pl.when((t == 0) | (odd_ref[i] != odd_ref[prev]))
    def _():
        xe_odd[...] = jnp.zeros(xe_odd.shape, xe_odd.dtype)

    n_sub = bs // SUB_TILE
    base = e * (nt * n_sub + 1) + t * n_sub
    first = [st_ref[base + u] for u in range(n_sub)]
    last = [st_ref[base + u + 1] for u in range(n_sub)]

    def gather(first_slot, n_rows, u):
        slot = first_slot + lax.broadcasted_iota(jnp.int32, (n_rows, 1), 0)
        tokens = slice(u * SUB_TILE, (u + 1) * SUB_TILE)
        onehot = jnp.where(pos_ref[:, tokens] == slot, 1.0, 0.0).astype(BF16)
        return jnp.dot(onehot, h_ref[tokens, :], preferred_element_type=F32).astype(xe_even.dtype)

    def add_rows(first_slot, rows, align):
        n_rows = rows.shape[0]
        off = pl.multiple_of(first_slot % bs, align)
        even = (first_slot // bs) % 2 == 0
        zeros = jnp.zeros_like(rows)
        xe_even[pl.ds(off, n_rows), :] += jnp.where(even, rows, zeros)
        xe_odd[pl.ds(off, n_rows), :] += jnp.where(even, zeros, rows)

    window = [(s // PACKED_ROWS) * PACKED_ROWS for s in first]
    fits = last[0] - window[0] <= DISPATCH_WINDOW
    for u in range(1, n_sub):
        fits = fits & (last[u] - window[u] <= DISPATCH_WINDOW)

    @pl.when(fits)
    def _():
        for u in range(n_sub):
            val = gather(window[u], DISPATCH_WINDOW, u)
            for q in range(0, DISPATCH_WINDOW, PACKED_ROWS):
                add_rows(window[u] + q, val[q:q + PACKED_ROWS], PACKED_ROWS)

    @pl.when(jnp.logical_not(fits))
    def _():
        for u in range(n_sub):
            w0 = (first[u] // SLOT_CHUNK) * SLOT_CHUNK
            val = gather(w0, 2 * SLOT_CHUNK, u)
            add_rows(w0, val[:SLOT_CHUNK], SLOT_CHUNK)
            add_rows(w0 + SLOT_CHUNK, val[SLOT_CHUNK:], SLOT_CHUNK)
        for u in range(n_sub):
            w2 = (first[u] // SLOT_CHUNK + 2) * SLOT_CHUNK

            @pl.when(last[u] > w2)
            def _(u=u, w2=w2):
                add_rows(w2, gather(w2, SLOT_CHUNK, u), SLOT_CHUNK)


def _dispatch(h2, pos, cap, bs):
    n, d = h2.shape
    e = pos.shape[0]
    tile = bs
    nt = n // tile
    nb = cap // bs
    assert cap % (2 * bs) == 0 and n % tile == 0 and bs % SUB_TILE == 0
    starts = _tile_starts(pos, SUB_TILE)
    b0 = starts[:, :-1:bs // SUB_TILE] // bs
    even_idx = ((b0 + 1) // 2).reshape(-1)
    odd_idx = (b0 // 2).reshape(-1)
    grp = DISPATCH_GROUP
    assert e % grp == 0

    def out_spec(x, parity):
        def index(gi, t, ev, od, st):
            idx = (gi * grp + x) * nt + t
            return (gi, (od if parity else ev)[idx], 0, 0)
        return pl.BlockSpec((None, None, bs, d), index)

    outs = pl.pallas_call(
        functools.partial(_dispatch_kernel, nt=nt, bs=bs),
        grid_spec=pltpu.PrefetchScalarGridSpec(
            num_scalar_prefetch=3, grid=(e // grp, nt),
            in_specs=[pl.BlockSpec((grp, 1, tile), lambda gi, t, *_: (gi, 0, t)),
                      pl.BlockSpec((tile, d), lambda gi, t, *_: (t, 0))],
            out_specs=[out_spec(x, parity) for x in range(grp) for parity in range(2)]),
        out_shape=[jax.ShapeDtypeStruct((e // grp, nb // 2 + 1 - parity, bs, d), BF16)
                   for x in range(grp) for parity in range(2)],
        compiler_params=_cparams(("arbitrary", "arbitrary")),
    )(even_idx, odd_idx, starts.reshape(-1), pos.reshape(e, 1, n), h2)
    return [(outs[2 * x], outs[2 * x + 1]) for x in range(grp)]


def _combine_kernel(st_ref, x1_ref, pos_ref, aff_ref, *refs, nt, by):
    y_ref, stage_ref = refs[-2], refs[-1]
    t = pl.program_id(0)
    first = [st_ref[e * (nt + 1) + t] for e in range(N_EXPERTS)]
    last = [st_ref[e * (nt + 1) + t + 1] for e in range(N_EXPERTS)]
    window = [(s // PACKED_ROWS) * PACKED_ROWS for s in first]
    fits = last[0] - window[0] <= COMBINE_WINDOW
    for e in range(1, N_EXPERTS):
        fits = fits & (last[e] - window[e] <= COMBINE_WINDOW)

    @pl.when(fits)
    def _():
        per_group = stage_ref.shape[1] // COMBINE_WINDOW
        lane = lax.broadcasted_iota(jnp.int32, (1, stage_ref.shape[1]), 1)
        acc = x1_ref[...]
        for g in range(N_EXPERTS // per_group):
            weights = jnp.zeros((x1_ref.shape[0], stage_ref.shape[1]), F32)
            for j in range(per_group):
                e = g * per_group + j
                blk_a, blk_b = refs[2 * e], refs[2 * e + 1]
                rel = window[e] - (first[e] // by) * by
                for q in range(COMBINE_WINDOW // PACKED_ROWS):
                    r = rel + q * PACKED_ROWS
                    ra = pl.multiple_of(jnp.minimum(r, by - PACKED_ROWS), PACKED_ROWS)
                    rb = pl.multiple_of(jnp.clip(r - by, 0, by - PACKED_ROWS), PACKED_ROWS)
                    rows = jnp.where(r < by, blk_a[pl.ds(ra, PACKED_ROWS), :],
                                     blk_b[pl.ds(rb, PACKED_ROWS), :])
                    dst = j * COMBINE_WINDOW + q * PACKED_ROWS
                    stage_ref[g, dst:dst + PACKED_ROWS, :] = rows
                offset = pos_ref[:, e:e + 1] - window[e]
                inside = (offset >= 0) & (offset < COMBINE_WINDOW)
                target = jnp.where(inside, offset + j * COMBINE_WINDOW, -1)
                weights = jnp.where(lane == target, aff_ref[:, e:e + 1], weights)
            acc = acc + jnp.dot(weights.astype(BF16), stage_ref[g],
                                preferred_element_type=F32)
        y_ref[...] = acc

    def contribution(e, j):
        lo = (first[e] // by + j) * by
        slot = lo + lax.broadcasted_iota(jnp.int32, (1, by), 1)
        onehot = jnp.where(pos_ref[:, e:e + 1] == slot, 1.0, 0.0).astype(BF16)
        part = jnp.dot(onehot, refs[2 * e + j][...], preferred_element_type=F32)
        return aff_ref[:, e:e + 1] * part

    @pl.when(jnp.logical_not(fits))
    def _():
        acc = x1_ref[...]
        for e in range(N_EXPERTS):
            acc = acc + contribution(e, 0)
        y_ref[...] = acc
        for e in range(N_EXPERTS):
            @pl.when(last[e] > (first[e] // by + 1) * by)
            def _(e=e):
                y_ref[...] += contribution(e, 1)


def _combine(x1, pos_tok, aff_tok, ye_groups, starts, tile):
    n, d = x1.shape
    grp = len(ye_groups)
    _, cap, _ = ye_groups[0].shape
    e = pos_tok.shape[1]
    by = tile
    nt = n // tile
    nby = cap // by

    def ye_spec(ei, j):
        def index(t, st):
            return (ei // grp, jnp.minimum(st[ei * (nt + 1) + t] // by + j, nby - 1), 0)
        return pl.BlockSpec((None, by, d), index)

    tok = lambda width: pl.BlockSpec((tile, width), lambda t, st: (t, 0))
    ye_specs = [ye_spec(ei, j) for ei in range(e) for j in range(2)]
    return pl.pallas_call(
        functools.partial(_combine_kernel, nt=nt, by=by),
        grid_spec=pltpu.PrefetchScalarGridSpec(
            num_scalar_prefetch=1, grid=(nt,),
            in_specs=[tok(d), tok(e), tok(e)] + ye_specs,
            out_specs=tok(d),
            scratch_shapes=[pltpu.VMEM((e * COMBINE_WINDOW // MXU_DEPTH, MXU_DEPTH, d), BF16)]),
        out_shape=jax.ShapeDtypeStruct((n, d), F32),
        compiler_params=_cparams(("arbitrary",)),
    )(starts.reshape(-1), x1, pos_tok, aff_tok,
      *[ye_groups[ei % grp] for ei in range(e) for _ in range(2)])


def _expert_kernel(xe_even, xe_odd, wg_ref, wu_ref, wd_ref, ye_ref):
    xe = jnp.where(pl.program_id(1) % 2 == 0, xe_even[...], xe_odd[...])
    a = jnp.dot(xe, wg_ref[...], preferred_element_type=F32)
    u = jnp.dot(xe, wu_ref[...], preferred_element_type=F32)
    hid = (a * jax.nn.sigmoid(a) * u).astype(BF16)
    ye_ref[...] = jnp.dot(hid, wd_ref[...], preferred_element_type=F32).astype(ye_ref.dtype)


def _expert_ffn(xe_even, xe_odd, w, x):
    e, nbh, bs, d = xe_odd.shape
    f = w['w_gate'].shape[2]
    xspec = pl.BlockSpec((None, None, bs, d), lambda ei, ci: (ei, ci // 2, 0, 0))
    wspec = lambda r, cc: pl.BlockSpec((None, r, cc),
                                       lambda ei, ci: (ei * DISPATCH_GROUP + x, 0, 0))
    return pl.pallas_call(
        _expert_kernel, grid=(e, 2 * nbh),
        in_specs=[xspec, xspec, wspec(d, f), wspec(d, f), wspec(f, d)],
        out_specs=pl.BlockSpec((None, bs, d), lambda ei, ci: (ei, ci, 0)),
        out_shape=jax.ShapeDtypeStruct((e, 2 * nbh * bs, d), BF16),
        compiler_params=_cparams(("parallel", "parallel")),
    )(xe_even, xe_odd, w['w_gate'], w['w_up'], w['w_down'])


def _tile(n, pref):
    t = min(n, pref)
    assert n % t == 0
    return t


def _encoder_layer(x, w):
    b, s, d = x.shape
    n = b * s
    tk = _tile(s, 1024)
    tables = _rope_tables(s)
    q, k, vt, qm, km, vmt = _pre_attention(x, tables, w, tk)
    tq = _tile(s, 4096)
    oa = _attention(q, k, vt, _GQA_MAPS, tq)
    ob = _attention(qm, km, vmt, _MLA_MAPS, tq)
    x1, h2, afft, aff_tok = _post_attention(x, oa, ob, w, _tile(s, 512))

    cap = CAPACITY_FACTOR * n // N_EXPERTS
    pos = _select(afft, cap)
    xe_groups = _dispatch(h2.reshape(n, d), pos, cap, _tile(cap // 2, 1024))
    ye_groups = [_expert_ffn(xe_even, xe_odd, w, x)
                 for x, (xe_even, xe_odd) in enumerate(xe_groups)]
    ctile = _tile(cap // 2, 256)
    y = _combine(x1.reshape(n, d), pos.T, aff_tok.reshape(n, N_EXPERTS), ye_groups,
                 _tile_starts(pos, ctile), ctile)
    return y.reshape(b, s, d)


def kernel(x_prompt, x_sample, norm_attn, w_in, gqa_q_norm, gqa_k_norm, mla_q_a_norm, mla_w_uq,
           mla_q_nope_norm, mla_q_pe_norm, mla_kv_a_norm, mla_w_ukv, mla_k_nope_norm,
           mla_k_pe_norm, out_norm_a, out_norm_b, w_o, norm_ffn, w_router, w_gate, w_up, w_down):
    p = dict(norm_attn=norm_attn, w_in=w_in, gqa_q_norm=gqa_q_norm, gqa_k_norm=gqa_k_norm,
             mla_q_a_norm=mla_q_a_norm, mla_w_uq=mla_w_uq, mla_q_nope_norm=mla_q_nope_norm,
             mla_q_pe_norm=mla_q_pe_norm, mla_kv_a_norm=mla_kv_a_norm, mla_w_ukv=mla_w_ukv,
             mla_k_nope_norm=mla_k_nope_norm, mla_k_pe_norm=mla_k_pe_norm,
             out_norm_a=out_norm_a, out_norm_b=out_norm_b, w_o=w_o, norm_ffn=norm_ffn,
             w_router=w_router, w_gate=w_gate, w_up=w_up, w_down=w_down)
    w = _prep_weights(p)
    return (_encoder_layer(x_prompt, w), _encoder_layer(x_sample, w))
```

```python
import functools
import math

import jax
import jax.numpy as jnp
from jax import lax
from jax.experimental import pallas as pl
from jax.experimental.pallas import tpu as pltpu

D_MODEL = 1024
GRID_W = 64
ROPE_THETA = 10000.0
RMS_EPS = 1e-6
GQA_HEADS = 8
GQA_KV_HEADS = 2
GQA_HEAD_DIM = 64
MLA_HEADS = 8
MLA_Q_LORA = 256
MLA_KV_LORA = 128
MLA_NOPE = 64
MLA_ROPE = 32
MLA_V = 64
N_EXPERTS = 16
CAPACITY_FACTOR = 2
D_FF_EXPERT = 1024

LANES = 128
HALF = LANES // 2
D_IN_PAD = 1280
VMEM_LIMIT = 56 * 1024 * 1024
LOG2E = math.log2(math.e)
ONES_ROWS = 16
MAX_LAG_EXPONENT = 60.0

F32 = jnp.float32
BF16 = jnp.bfloat16


def _cparams(sem):
    return pltpu.CompilerParams(dimension_semantics=sem, vmem_limit_bytes=VMEM_LIMIT)


def _rope_tables(seq_len):
    rows = seq_len // GRID_W
    row = jnp.repeat(jnp.arange(rows, dtype=F32), GRID_W)
    col = jnp.tile(jnp.arange(GRID_W, dtype=F32), rows)

    def angles(rot_dim):
        half = rot_dim // 2
        inv = ROPE_THETA ** (-jnp.arange(0, half, 2, dtype=F32) / half)
        ar = row[:, None] * inv[None, :]
        ac = col[:, None] * inv[None, :]
        return jnp.concatenate([ar, ar, ac, ac], axis=-1)

    lane = jnp.arange(LANES)
    a64 = angles(GQA_HEAD_DIM)
    cos_a = jnp.tile(jnp.cos(a64), (1, 2))
    sin_a = jnp.tile(jnp.sin(a64), (1, 2))
    low_a = (lane % 32) < 16
    up_a = jnp.where(low_a, -sin_a, 0.0)
    dn_a = jnp.where(low_a, 0.0, sin_a)

    a32 = angles(MLA_ROPE)
    pad = ((0, 0), (MLA_NOPE, LANES - MLA_NOPE - MLA_ROPE))
    cos_b = jnp.pad(jnp.cos(a32) - 1.0, pad) + 1.0
    sin_b = jnp.pad(jnp.sin(a32), pad)
    low_b = ((lane - MLA_NOPE) % 16) < 8
    up_b = jnp.where(low_b, -sin_b, 0.0)
    dn_b = jnp.where(low_b, 0.0, sin_b)
    return jnp.stack([cos_a, up_a, dn_a, cos_b, up_b, dn_b])


def _segment_mean_matrix(groups):
    m = jnp.zeros((2 * LANES, 2 * LANES), F32)
    for base in (0, LANES):
        for start, width in groups:
            lo = base + start
            m = m.at[lo:lo + width, lo:lo + width].set(1.0 / width)
    return m.astype(BF16)


def _prep_weights(p):
    w_in = p['w_in']
    kpe_cols = jnp.pad(w_in[:, 1152:1184], ((0, 0), (MLA_NOPE, LANES - MLA_NOPE - MLA_ROPE)))
    w_in_p = jnp.concatenate([w_in[:, :1152], kpe_cols], axis=1).astype(BF16)
    w_uq = p['mla_w_uq'].reshape(MLA_Q_LORA, MLA_HEADS, MLA_NOPE + MLA_ROPE)
    w_uq_p = jnp.pad(w_uq, ((0, 0), (0, 0), (0, LANES - MLA_NOPE - MLA_ROPE)))
    w_uq_p = w_uq_p.reshape(MLA_Q_LORA, MLA_HEADS * LANES).astype(BF16)
    w_ukv = p['mla_w_ukv'].reshape(MLA_KV_LORA, MLA_HEADS, MLA_NOPE + MLA_V)
    w_uk_p = jnp.pad(w_ukv[:, :, :MLA_NOPE], ((0, 0), (0, 0), (0, LANES - MLA_NOPE)))
    w_uk_p = w_uk_p.reshape(MLA_KV_LORA, MLA_HEADS * LANES).astype(BF16)
    w_uv = w_ukv[:, :, MLA_NOPE:].reshape(MLA_KV_LORA, MLA_HEADS * MLA_V).astype(BF16)

    def row(v):
        return v.reshape(1, -1).astype(F32)

    zeros32 = jnp.zeros((LANES - MLA_NOPE - MLA_ROPE,), F32)
    zeros64 = jnp.zeros((HALF,), F32)
    return dict(
        w_in=w_in_p, w_uq=w_uq_p, w_uk=w_uk_p, w_uv=w_uv,
        g_attn=row(p['norm_attn']),
        g_q=row(jnp.tile(p['gqa_q_norm'], 2)),
        g_k=row(jnp.tile(p['gqa_k_norm'], 2)),
        g_cq=row(p['mla_q_a_norm']),
        g_qm=row(jnp.concatenate([p['mla_q_nope_norm'], p['mla_q_pe_norm'], zeros32])),
        g_ckv=row(p['mla_kv_a_norm']),
        g_kn=row(jnp.concatenate([p['mla_k_nope_norm'], zeros64])),
        g_kpe=row(jnp.concatenate([zeros64, p['mla_k_pe_norm'], zeros32])),
        s_gqa=_segment_mean_matrix([(0, HALF), (HALF, HALF)]),
        s_mla=_segment_mean_matrix([(0, MLA_NOPE), (MLA_NOPE, MLA_ROPE)]),
        g_oa=row(p['out_norm_a']), g_ob=row(p['out_norm_b']),
        w_o=p['w_o'].astype(BF16),
        g_ffn=row(p['norm_ffn']),
        w_r=jnp.pad(p['w_router'], ((0, 0), (0, LANES - N_EXPERTS))).astype(BF16),
        w_gate=p['w_gate'].astype(BF16), w_up=p['w_up'].astype(BF16),
        w_down=p['w_down'].astype(BF16),
    )


def _rms(x, gain):
    ms = jnp.mean(x * x, axis=-1, keepdims=True)
    return x * lax.rsqrt(ms + RMS_EPS) * gain


def _segment_inv_rms(x, seg_mat):
    width = seg_mat.shape[0]
    sq = (x * x).astype(BF16)
    ms = [jnp.dot(sq[:, c:c + width], seg_mat, preferred_element_type=F32)
          for c in range(0, x.shape[1], width)]
    return lax.rsqrt((ms[0] if len(ms) == 1 else jnp.concatenate(ms, axis=1)) + RMS_EPS)


def _rope(x, cos, up, dn, quarter):
    return (x * cos + pltpu.roll(x, LANES - quarter, 1) * up + pltpu.roll(x, quarter, 1) * dn)


def _pre_attn_kernel(x_ref, tab_ref, g_attn, w_in, g_q, g_k, g_cq, w_uq, g_qm, g_ckv, w_uk, w_uv,
                     g_kn, g_kpe, s_gqa, s_mla,
                     q_ref, k_ref, v_ref, qm_ref, km_ref, vm_ref):
    cos_a, up_a, dn_a = tab_ref[0], tab_ref[1], tab_ref[2]
    cos_b, up_b, dn_b = tab_ref[3], tab_ref[4], tab_ref[5]
    hn = _rms(x_ref[...], g_attn[...]).astype(BF16)
    proj = jnp.dot(hn, w_in[...], preferred_element_type=F32)

    lanes = lambda a, c: a[:, c * LANES:(c + 1) * LANES]
    sg = s_gqa[...]
    sm = s_mla[...]
    q_scale = LOG2E / math.sqrt(GQA_HEAD_DIM)
    gq = proj[:, :512]
    gq_inv = _segment_inv_rms(gq, sg)
    for c in range(GQA_HEADS // 2):
        xc = lanes(gq, c) * lanes(gq_inv, c) * g_q[...]
        q_ref[c] = (_rope(xc, cos_a, up_a, dn_a, 16) * q_scale).astype(BF16)

    low = lax.broadcasted_iota(jnp.int32, (1, LANES), 1) < HALF
    gk = proj[:, 512:640]
    k01 = _rope(gk * _segment_inv_rms(gk, s_gqa[:LANES, :LANES]) * g_k[...], cos_a, up_a, dn_a, 16)
    k10 = pltpu.roll(k01, HALF, 1)
    k_ref[0] = jnp.where(low, k01, 0.0).astype(BF16)
    k_ref[1] = jnp.where(low, 0.0, k10).astype(BF16)
    k_ref[2] = jnp.where(low, k10, 0.0).astype(BF16)
    k_ref[3] = jnp.where(low, 0.0, k01).astype(BF16)
    v01 = proj[:, 640:768]
    v_ref[0] = jnp.transpose(v01).astype(BF16)
    v_ref[1] = jnp.transpose(pltpu.roll(v01, HALF, 1)).astype(BF16)

    cq = _rms(proj[:, 768:1024], g_cq[...]).astype(BF16)
    qb = jnp.dot(cq, w_uq[...], preferred_element_type=F32)
    m_scale = LOG2E / math.sqrt(MLA_NOPE + MLA_ROPE)
    ckv = _rms(proj[:, 1024:1152], g_ckv[...]).astype(BF16)
    kn = jnp.dot(ckv, w_uk[...], preferred_element_type=F32)
    vv = jnp.dot(ckv, w_uv[...], preferred_element_type=F32)
    gkpe = proj[:, 1152:1280]
    kpe = _rope(gkpe * _segment_inv_rms(gkpe, s_mla[:LANES, :LANES]) * g_kpe[...],
                cos_b, up_b, dn_b, 8)
    qb_inv = _segment_inv_rms(qb, sm)
    kn_inv = _segment_inv_rms(kn, sm)
    for h in range(MLA_HEADS):
        qh = lanes(qb, h) * lanes(qb_inv, h) * g_qm[...]
        qm_ref[h] = (_rope(qh, cos_b, up_b, dn_b, 8) * m_scale).astype(BF16)
        km_ref[h] = (lanes(kn, h) * lanes(kn_inv, h) * g_kn[...] + kpe).astype(BF16)
    for c in range(MLA_HEADS // 2):
        vm_ref[c] = jnp.transpose(vv[:, c * LANES:(c + 1) * LANES]).astype(BF16)


def _full(shape):
    return pl.BlockSpec(shape, lambda *_: (0,) * len(shape))


def _pre_attention(x, tables, w, tm):
    b, s, d = x.shape
    nt = s // tm
    grid = (b, nt)

    def head_major(n):
        return pl.BlockSpec((None, n, tm, LANES), lambda bi, i: (bi, 0, i, 0))

    weights = [w['g_attn'], w['w_in'], w['g_q'], w['g_k'], w['g_cq'], w['w_uq'], w['g_qm'],
               w['g_ckv'], w['w_uk'], w['w_uv'], w['g_kn'], w['g_kpe'], w['s_gqa'], w['s_mla']]
    in_specs = [pl.BlockSpec((None, tm, d), lambda bi, i: (bi, i, 0)),
                pl.BlockSpec((6, tm, LANES), lambda bi, i: (0, i, 0))]
    in_specs += [_full(a.shape) for a in weights]
    def head_major_t(n):
        return pl.BlockSpec((None, n, None, LANES, tm), lambda bi, i: (bi, 0, i, 0, 0))

    counts = (4, 4, 2, 8, 8, 4)
    transposed = (False, False, True, False, False, True)
    out_shape = [jax.ShapeDtypeStruct((b, n, nt, LANES, tm) if t else (b, n, s, LANES), BF16)
                 for n, t in zip(counts, transposed)]
    out_specs = [head_major_t(n) if t else head_major(n) for n, t in zip(counts, transposed)]
    return pl.pallas_call(
        _pre_attn_kernel, grid=grid, in_specs=in_specs, out_specs=out_specs, out_shape=out_shape,
        compiler_params=_cparams(("parallel", "parallel")),
    )(x, tables, *weights)


def _attn_kernel(qe_ref, qo_ref, ke_ref, ko_ref, ve_ref, vo_ref, o_ref, acc_sc, *, tk):
    n_chunks = ve_ref.shape[0]
    tq = acc_sc.shape[2]
    refs = ((qe_ref, ke_ref, ve_ref), (qo_ref, ko_ref, vo_ref))
    qts = [jnp.transpose(q_ref[...].astype(F32)).astype(BF16) for q_ref, _, _ in refs]
    ones_rows = jnp.ones((ONES_ROWS, tk), BF16)

    def scores(j):
        start = pl.multiple_of(j * tk, tk)
        return [jnp.dot(k_ref[pl.ds(start, tk), :], qts[h], preferred_element_type=F32)
                for h, (_, k_ref, _) in enumerate(refs)]

    def weighted_values(h, j, p):
        v_ref = refs[h][2]
        lhs = jnp.concatenate([v_ref[j][h * HALF:(h + 1) * HALF], ones_rows], axis=0)
        return jnp.dot(lhs, p.astype(BF16), preferred_element_type=F32)

    def lagged_body(j, carry):
        sts = scores(j)
        out = []
        for h in range(2):
            ref_max, excess = carry[h]
            chunk_max = jnp.max(sts[h], axis=0, keepdims=True)
            p = jnp.exp2(sts[h] - ref_max)
            new_max = jnp.maximum(ref_max, chunk_max)
            acc_sc[h] = jnp.exp2(ref_max - new_max) * (acc_sc[h] + weighted_values(h, j, p))
            out.append((new_max, jnp.maximum(excess, chunk_max - ref_max)))
        return tuple(out)

    def exact_body(j, carry):
        sts = scores(j)
        out = []
        for h in range(2):
            new_max = jnp.maximum(carry[h], jnp.max(sts[h], axis=0, keepdims=True))
            p = jnp.exp2(sts[h] - new_max)
            acc_sc[h] = jnp.exp2(carry[h] - new_max) * acc_sc[h] + weighted_values(h, j, p)
            out.append(new_max)
        return tuple(out)

    def run(body, init):
        acc_sc[...] = jnp.zeros(acc_sc.shape, F32)
        carry = lax.fori_loop(0, n_chunks, body, init)
        ot = jnp.concatenate([acc_sc[h, :HALF] / acc_sc[h, HALF:HALF + 1] for h in range(2)],
                             axis=0)
        o_ref[...] = jnp.transpose(ot)
        return carry

    first_ref = [jnp.max(jnp.dot(k_ref[0:16, :], qts[h], preferred_element_type=F32),
                         axis=0, keepdims=True) for h, (_, k_ref, _) in enumerate(refs)]
    zero = jnp.zeros((1, tq), F32)
    (_, x_even), (_, x_odd) = run(lagged_body, tuple((r, zero) for r in first_ref))
    safe = jnp.max(jnp.maximum(x_even, x_odd)) < MAX_LAG_EXPONENT

    @pl.when(jnp.logical_not(safe))
    def _():
        neg_inf = jnp.full((1, tq), -jnp.inf, F32)
        run(exact_body, (neg_inf, neg_inf))


def _attention(q_arr, k_arr, vt_arr, maps, tq):
    b, _, s, _ = q_arr.shape
    n_pairs = 4
    n_chunks, _, tk = vt_arr.shape[2:]
    grid = (b, n_pairs, s // tq)
    qe, qo, ke, ko, ve, vo = maps

    def qspec(f):
        return pl.BlockSpec((None, None, tq, LANES), lambda bi, p, i: (bi, f(p), i, 0))

    def kspec(f):
        return pl.BlockSpec((None, None, s, LANES), lambda bi, p, i: (bi, f(p), 0, 0))

    def vspec(f):
        return pl.BlockSpec((None, None, n_chunks, LANES, tk),
                            lambda bi, p, i: (bi, f(p), 0, 0, 0))

    return pl.pallas_call(
        functools.partial(_attn_kernel, tk=tk), grid=grid,
        in_specs=[qspec(qe), qspec(qo), kspec(ke), kspec(ko), vspec(ve), vspec(vo)],
        out_specs=pl.BlockSpec((None, tq, LANES), lambda bi, p, i: (bi, i, p)),
        out_shape=jax.ShapeDtypeStruct((b, s, n_pairs * LANES), F32),
        scratch_shapes=[pltpu.VMEM((2, HALF + ONES_ROWS, tq), F32)],
        compiler_params=_cparams(("parallel", "parallel", "parallel")),
    )(q_arr, q_arr, k_arr, k_arr, vt_arr, vt_arr)


_GQA_MAPS = (lambda p: p, lambda p: p,
             lambda p: 2 * (p // 2), lambda p: 2 * (p // 2) + 1,
             lambda p: p // 2, lambda p: 1 - p // 2)
_MLA_MAPS = (lambda p: 2 * p, lambda p: 2 * p + 1,
             lambda p: 2 * p, lambda p: 2 * p + 1,
             lambda p: p, lambda p: p)


def _post_attn_kernel(x_ref, oa_ref, ob_ref, g_oa, g_ob, w_o, g_ffn, w_r,
                      x1_ref, h2_ref, afft_ref, aff_ref):
    na = _rms(oa_ref[...], g_oa[...]).astype(BF16)
    nb = _rms(ob_ref[...], g_ob[...]).astype(BF16)
    half = na.shape[1]
    mix = (jnp.dot(na, w_o[:half, :], preferred_element_type=F32)
           + jnp.dot(nb, w_o[half:, :], preferred_element_type=F32))
    x1 = x_ref[...] + mix
    x1_ref[...] = x1
    h2 = _rms(x1, g_ffn[...]).astype(BF16)
    h2_ref[...] = h2
    logits = jnp.dot(h2, w_r[...], preferred_element_type=F32)
    valid = lax.broadcasted_iota(jnp.int32, (1, LANES), 1) < N_EXPERTS
    logits = jnp.where(valid, logits, -jnp.inf)
    e = jnp.exp(logits - jnp.max(logits, axis=-1, keepdims=True))
    aff = e / jnp.sum(e, axis=-1, keepdims=True)
    afft_ref[...] = jnp.transpose(aff)[:N_EXPERTS, :]
    aff_ref[...] = aff[:, :N_EXPERTS]


def _post_attention(x, oa, ob, w, tm):
    b, s, d = x.shape
    nt = s // tm
    tok = lambda width: pl.BlockSpec((None, tm, width), lambda bi, i: (bi, i, 0))
    weights = [w['g_oa'], w['g_ob'], w['w_o'], w['g_ffn'], w['w_r']]
    return pl.pallas_call(
        _post_attn_kernel, grid=(b, nt),
        in_specs=[tok(d), tok(oa.shape[2]), tok(ob.shape[2])] + [_full(a.shape) for a in weights],
        out_specs=[tok(d), tok(d),
                   pl.BlockSpec((N_EXPERTS, tm), lambda bi, i: (0, bi * nt + i)),
                   tok(N_EXPERTS)],
        out_shape=[jax.ShapeDtypeStruct((b, s, d), F32), jax.ShapeDtypeStruct((b, s, d), BF16),
                   jax.ShapeDtypeStruct((N_EXPERTS, b * s), F32),
                   jax.ShapeDtypeStruct((b, s, N_EXPERTS), F32)],
        compiler_params=_cparams(("parallel", "parallel")),
    )(x, oa, ob, *weights)


MXU_DEPTH = 256
SLOT_CHUNK = LANES
SUB_TILE = MXU_DEPTH
PACKED_ROWS = 16
COMBINE_WINDOW = 64
DISPATCH_WINDOW = 64
DISPATCH_GROUP = 4


def _threshold_kernel(aff_ref, thr_ref, need_ref, *, cap):
    def count(mask):
        return jnp.sum(jnp.where(mask, 1.0, 0.0), axis=1, keepdims=True)

    def body(i, prefix):
        cand = prefix | jnp.left_shift(jnp.int32(1), 30 - i)
        cnt = count(pltpu.bitcast(aff_ref[...], jnp.int32) >= cand)
        return jnp.where(cnt >= cap, cand, prefix)

    thr = lax.fori_loop(0, 31, body, jnp.zeros((N_EXPERTS, 1), jnp.int32))
    greater = count(pltpu.bitcast(aff_ref[...], jnp.int32) > thr)
    thr_ref[...] = thr
    need_ref[...] = (cap - greater).astype(jnp.int32)


def _position_kernel(thr_ref, need_ref, aff_ref, upper_ref, lower_ref, pos_ref):
    e = pl.program_id(0)
    bits = pltpu.bitcast(aff_ref[...], jnp.int32)
    thr = thr_ref[e]
    need = need_ref[e].astype(F32)
    upper = upper_ref[...]
    lower = lower_ref[...]

    def inclusive_cumsum(mask):
        m = jnp.where(mask, 1.0, 0.0).astype(BF16)
        within = jnp.dot(m, upper, preferred_element_type=F32)
        above = jnp.dot(lower, m, preferred_element_type=F32)
        return within + jnp.sum(above, axis=1, keepdims=True)

    greater = bits > thr
    equal = bits == thr
    equal_rank = inclusive_cumsum(equal) - jnp.where(equal, 1.0, 0.0)
    selected = greater | (equal & (equal_rank < need))
    pos = inclusive_cumsum(selected) - 1.0
    pos_ref[...] = jnp.where(selected, pos, -1.0).astype(jnp.int32)


def _select(afft, cap):
    e, n = afft.shape
    nb = n // LANES
    thr, need = pl.pallas_call(
        functools.partial(_threshold_kernel, cap=cap),
        out_shape=[jax.ShapeDtypeStruct((e, 1), jnp.int32)] * 2,
        compiler_params=pltpu.CompilerParams(vmem_limit_bytes=VMEM_LIMIT),
    )(afft)
    upper = jnp.triu(jnp.ones((LANES, LANES), F32)).astype(BF16)
    lower = jnp.tril(jnp.ones((nb, nb), F32), -1).astype(BF16)
    pos = pl.pallas_call(
        _position_kernel,
        grid_spec=pltpu.PrefetchScalarGridSpec(
            num_scalar_prefetch=2, grid=(e,),
            in_specs=[pl.BlockSpec((nb, LANES), lambda ei, *_: (ei, 0)),
                      pl.BlockSpec((LANES, LANES), lambda ei, *_: (0, 0)),
                      pl.BlockSpec((nb, nb), lambda ei, *_: (0, 0))],
            out_specs=pl.BlockSpec((nb, LANES), lambda ei, *_: (ei, 0))),
        out_shape=jax.ShapeDtypeStruct((e * nb, LANES), jnp.int32),
        compiler_params=_cparams(("parallel",)),
    )(thr.reshape(e), need.reshape(e), afft.reshape(e * nb, LANES), upper, lower)
    return pos.reshape(e, n)


def _tile_starts(pos, tile):
    e, n = pos.shape
    cnt = jnp.sum((pos >= 0).reshape(e, n // tile, tile), axis=-1, dtype=jnp.int32)
    return jnp.concatenate([jnp.zeros((e, 1), jnp.int32), jnp.cumsum(cnt, axis=1)], axis=1)


def _dispatch_kernel(even_ref, odd_ref, st_ref, pos_ref, h_ref, *outs, nt, bs):
    for x in range(DISPATCH_GROUP):
        _dispatch_expert(pl.program_id(0) * DISPATCH_GROUP + x, even_ref, odd_ref, st_ref,
                         pos_ref.at[x], h_ref, outs[2 * x], outs[2 * x + 1], nt=nt, bs=bs)


def _dispatch_expert(e, even_ref, odd_ref, st_ref, pos_ref, h_ref, xe_even, xe_odd, *, nt, bs):
    t = pl.program_id(1)
    i = e * nt + t
    prev = jnp.maximum(i - 1, 0)

    @pl.when((t == 0) | (even_ref[i] != even_ref[prev]))
    def _():
        xe_even[...] = jnp.zeros(xe_even.shape, xe_even.dtype)

    @pl.when((t == 0) | (odd_ref[i] != odd_ref[prev]))
    def _():
        xe_odd[...] = jnp.zeros(xe_odd.shape, xe_odd.dtype)

    n_sub = bs // SUB_TILE
    base = e * (nt * n_sub + 1) + t * n_sub
    first = [st_ref[base + u] for u in range(n_sub)]
    last = [st_ref[base + u + 1] for u in range(n_sub)]

    def gather(first_slot, n_rows, u):
        slot = first_slot + lax.broadcasted_iota(jnp.int32, (n_rows, 1), 0)
        tokens = slice(u * SUB_TILE, (u + 1) * SUB_TILE)
        onehot = jnp.where(pos_ref[:, tokens] == slot, 1.0, 0.0).astype(BF16)
        return jnp.dot(onehot, h_ref[tokens, :], preferred_element_type=F32).astype(xe_even.dtype)

    def add_rows(first_slot, rows, align):
        n_rows = rows.shape[0]
        off = pl.multiple_of(first_slot % bs, align)
        even = (first_slot // bs) % 2 == 0
        zeros = jnp.zeros_like(rows)
        xe_even[pl.ds(off, n_rows), :] += jnp.where(even, rows, zeros)
        xe_odd[pl.ds(off, n_rows), :] += jnp.where(even, zeros, rows)

    window = [(s // PACKED_ROWS) * PACKED_ROWS for s in first]
    fits = last[0] - window[0] <= DISPATCH_WINDOW
    for u in range(1, n_sub):
        fits = fits & (last[u] - window[u] <= DISPATCH_WINDOW)

    @pl.when(fits)
    def _():
        for u in range(n_sub):
            val = gather(window[u], DISPATCH_WINDOW, u)
            for q in range(0, DISPATCH_WINDOW, PACKED_ROWS):
                add_rows(window[u] + q, val[q:q + PACKED_ROWS], PACKED_ROWS)

    @pl.when(jnp.logical_not(fits))
    def _():
        for u in range(n_sub):
            w0 = (first[u] // SLOT_CHUNK) * SLOT_CHUNK
            val = gather(w0, 2 * SLOT_CHUNK, u)
            add_rows(w0, val[:SLOT_CHUNK], SLOT_CHUNK)
            add_rows(w0 + SLOT_CHUNK, val[SLOT_CHUNK:], SLOT_CHUNK)
        for u in range(n_sub):
            w2 = (first[u] // SLOT_CHUNK + 2) * SLOT_CHUNK

            @pl.when(last[u] > w2)
            def _(u=u, w2=w2):
                add_rows(w2, gather(w2, SLOT_CHUNK, u), SLOT_CHUNK)


def _dispatch(h2, pos, cap, bs):
    n, d = h2.shape
    e = pos.shape[0]
    tile = bs
    nt = n // tile
    nb = cap // bs
    assert cap % (2 * bs) == 0 and n % tile == 0 and bs % SUB_TILE == 0
    starts = _tile_starts(pos, SUB_TILE)
    b0 = starts[:, :-1:bs // SUB_TILE] // bs
    even_idx = ((b0 + 1) // 2).reshape(-1)
    odd_idx = (b0 // 2).reshape(-1)
    grp = DISPATCH_GROUP
    assert e % grp == 0

    def out_spec(x, parity):
        def index(gi, t, ev, od, st):
            idx = (gi * grp + x) * nt + t
            return (gi, (od if parity else ev)[idx], 0, 0)
        return pl.BlockSpec((None, None, bs, d), index)

    outs = pl.pallas_call(
        functools.partial(_dispatch_kernel, nt=nt, bs=bs),
        grid_spec=pltpu.PrefetchScalarGridSpec(
            num_scalar_prefetch=3, grid=(e // grp, nt),
            in_specs=[pl.BlockSpec((grp, 1, tile), lambda gi, t, *_: (gi, 0, t)),
                      pl.BlockSpec((tile, d), lambda gi, t, *_: (t, 0))],
            out_specs=[out_spec(x, parity) for x in range(grp) for parity in range(2)]),
        out_shape=[jax.ShapeDtypeStruct((e // grp, nb // 2 + 1 - parity, bs, d), BF16)
                   for x in range(grp) for parity in range(2)],
        compiler_params=_cparams(("arbitrary", "arbitrary")),
    )(even_idx, odd_idx, starts.reshape(-1), pos.reshape(e, 1, n), h2)
    return [(outs[2 * x], outs[2 * x + 1]) for x in range(grp)]


def _combine_kernel(st_ref, x1_ref, pos_ref, aff_ref, *refs, nt, by):
    y_ref, stage_ref = refs[-2], refs[-1]
    t = pl.program_id(0)
    first = [st_ref[e * (nt + 1) + t] for e in range(N_EXPERTS)]
    last = [st_ref[e * (nt + 1) + t + 1] for e in range(N_EXPERTS)]
    window = [(s // PACKED_ROWS) * PACKED_ROWS for s in first]
    fits = last[0] - window[0] <= COMBINE_WINDOW
    for e in range(1, N_EXPERTS):
        fits = fits & (last[e] - window[e] <= COMBINE_WINDOW)

    @pl.when(fits)
    def _():
        per_group = stage_ref.shape[1] // COMBINE_WINDOW
        lane = lax.broadcasted_iota(jnp.int32, (1, stage_ref.shape[1]), 1)
        acc = x1_ref[...]
        for g in range(N_EXPERTS // per_group):
            weights = jnp.zeros((x1_ref.shape[0], stage_ref.shape[1]), F32)
            for j in range(per_group):
                e = g * per_group + j
                blk_a, blk_b = refs[2 * e], refs[2 * e + 1]
                rel = window[e] - (first[e] // by) * by
                for q in range(COMBINE_WINDOW // PACKED_ROWS):
                    r = rel + q * PACKED_ROWS
                    ra = pl.multiple_of(jnp.minimum(r, by - PACKED_ROWS), PACKED_ROWS)
                    rb = pl.multiple_of(jnp.clip(r - by, 0, by - PACKED_ROWS), PACKED_ROWS)
                    rows = jnp.where(r < by, blk_a[pl.ds(ra, PACKED_ROWS), :],
                                     blk_b[pl.ds(rb, PACKED_ROWS), :])
                    dst = j * COMBINE_WINDOW + q * PACKED_ROWS
                    stage_ref[g, dst:dst + PACKED_ROWS, :] = rows
                offset = pos_ref[:, e:e + 1] - window[e]
                inside = (offset >= 0) & (offset < COMBINE_WINDOW)
                target = jnp.where(inside, offset + j * COMBINE_WINDOW, -1)
                weights = jnp.where(lane == target, aff_ref[:, e:e + 1], weights)
            acc = acc + jnp.dot(weights.astype(BF16), stage_ref[g],
                                preferred_element_type=F32)
        y_ref[...] = acc

    def contribution(e, j):
        lo = (first[e] // by + j) * by
        slot = lo + lax.broadcasted_iota(jnp.int32, (1, by), 1)
        onehot = jnp.where(pos_ref[:, e:e + 1] == slot, 1.0, 0.0).astype(BF16)
        part = jnp.dot(onehot, refs[2 * e + j][...], preferred_element_type=F32)
        return aff_ref[:, e:e + 1] * part

    @pl.when(jnp.logical_not(fits))
    def _():
        acc = x1_ref[...]
        for e in range(N_EXPERTS):
            acc = acc + contribution(e, 0)
        y_ref[...] = acc
        for e in range(N_EXPERTS):
            @pl.when(last[e] > (first[e] // by + 1) * by)
            def _(e=e):
                y_ref[...] += contribution(e, 1)


def _combine(x1, pos_tok, aff_tok, ye_groups, starts, tile):
    n, d = x1.shape
    grp = len(ye_groups)
    _, cap, _ = ye_groups[0].shape
    e = pos_tok.shape[1]
    by = tile
    nt = n // tile
    nby = cap // by

    def ye_spec(ei, j):
        def index(t, st):
            return (ei // grp, jnp.minimum(st[ei * (nt + 1) + t] // by + j, nby - 1), 0)
        return pl.BlockSpec((None, by, d), index)

    tok = lambda width: pl.BlockSpec((tile, width), lambda t, st: (t, 0))
    ye_specs = [ye_spec(ei, j) for ei in range(e) for j in range(2)]
    return pl.pallas_call(
        functools.partial(_combine_kernel, nt=nt, by=by),
        grid_spec=pltpu.PrefetchScalarGridSpec(
            num_scalar_prefetch=1, grid=(nt,),
            in_specs=[tok(d), tok(e), tok(e)] + ye_specs,
            out_specs=tok(d),
            scratch_shapes=[pltpu.VMEM((e * COMBINE_WINDOW // MXU_DEPTH, MXU_DEPTH, d), BF16)]),
        out_shape=jax.ShapeDtypeStruct((n, d), F32),
        compiler_params=_cparams(("arbitrary",)),
    )(starts.reshape(-1), x1, pos_tok, aff_tok,
      *[ye_groups[ei % grp] for ei in range(e) for _ in range(2)])


def _expert_kernel(xe_even, xe_odd, wg_ref, wu_ref, wd_ref, ye_ref):
    xe = jnp.where(pl.program_id(1) % 2 == 0, xe_even[...], xe_odd[...])
    a = jnp.dot(xe, wg_ref[...], preferred_element_type=F32)
    u = jnp.dot(xe, wu_ref[...], preferred_element_type=F32)
    hid = (a * jax.nn.sigmoid(a) * u).astype(BF16)
    ye_ref[...] = jnp.dot(hid, wd_ref[...], preferred_element_type=F32).astype(ye_ref.dtype)


def _expert_ffn(xe_even, xe_odd, w, x):
    e, nbh, bs, d = xe_odd.shape
    f = w['w_gate'].shape[2]
    xspec = pl.BlockSpec((None, None, bs, d), lambda ei, ci: (ei, ci // 2, 0, 0))
    wspec = lambda r, cc: pl.BlockSpec((None, r, cc),
                                       lambda ei, ci: (ei * DISPATCH_GROUP + x, 0, 0))
    return pl.pallas_call(
        _expert_kernel, grid=(e, 2 * nbh),
        in_specs=[xspec, xspec, wspec(d, f), wspec(d, f), wspec(f, d)],
        out_specs=pl.BlockSpec((None, bs, d), lambda ei, ci: (ei, ci, 0)),
        out_shape=jax.ShapeDtypeStruct((e, 2 * nbh * bs, d), BF16),
        compiler_params=_cparams(("parallel", "parallel")),
    )(xe_even, xe_odd, w['w_gate'], w['w_up'], w['w_down'])


def _tile(n, pref):
    t = min(n, pref)
    assert n % t == 0
    return t


def _encoder_layer(x, w):
    b, s, d = x.shape
    n = b * s
    tk = _tile(s, 1024)
    tables = _rope_tables(s)
    q, k, vt, qm, km, vmt = _pre_attention(x, tables, w, tk)
    tq = _tile(s, 4096)
    oa = _attention(q, k, vt, _GQA_MAPS, tq)
    ob = _attention(qm, km, vmt, _MLA_MAPS, tq)
    x1, h2, afft, aff_tok = _post_attention(x, oa, ob, w, _tile(s, 512))

    cap = CAPACITY_FACTOR * n // N_EXPERTS
    pos = _select(afft, cap)
    xe_groups = _dispatch(h2.reshape(n, d), pos, cap, _tile(cap // 2, 1024))
    ye_groups = [_expert_ffn(xe_even, xe_odd, w, x)
                 for x, (xe_even, xe_odd) in enumerate(xe_groups)]
    ctile = _tile(cap // 2, 256)
    y = _combine(x1.reshape(n, d), pos.T, aff_tok.reshape(n, N_EXPERTS), ye_groups,
                 _tile_starts(pos, ctile), ctile)
    return y.reshape(b, s, d)


def kernel(x_prompt, x_sample, norm_attn, w_in, gqa_q_norm, gqa_k_norm, mla_q_a_norm, mla_w_uq,
           mla_q_nope_norm, mla_q_pe_norm, mla_kv_a_norm, mla_w_ukv, mla_k_nope_norm,
           mla_k_pe_norm, out_norm_a, out_norm_b, w_o, norm_ffn, w_router, w_gate, w_up, w_down):
    p = dict(norm_attn=norm_attn, w_in=w_in, gqa_q_norm=gqa_q_norm, gqa_k_norm=gqa_k_norm,
             mla_q_a_norm=mla_q_a_norm, mla_w_uq=mla_w_uq, mla_q_nope_norm=mla_q_nope_norm,
             mla_q_pe_norm=mla_q_pe_norm, mla_kv_a_norm=mla_kv_a_norm, mla_w_ukv=mla_w_ukv,
             mla_k_nope_norm=mla_k_nope_norm, mla_k_pe_norm=mla_k_pe_norm,
             out_norm_a=out_norm_a, out_norm_b=out_norm_b, w_o=w_o, norm_ffn=norm_ffn,
             w_router=w_router, w_gate=w_gate, w_up=w_up, w_down=w_down)
    w = _prep_weights(p)
    return (_encoder_layer(x_prompt, w), _encoder_layer(x_sample, w))
```
